```python
import math
import jax, jax.numpy as jnp
from jax import lax
import numpy as np

D_MODEL = 2048
BATCH = 4
SEQ = 2048
DEPTH = 1

CTX_LEN = 256
GRID_W = 64
EPS = 1e-6
N_MOD = 6

SSD_HEADS = 32
SSD_HEAD_DIM = 64
D_SSD = SSD_HEADS * SSD_HEAD_DIM
SSD_GROUPS = 4
HEADS_PER_GROUP = SSD_HEADS // SSD_GROUPS
D_STATE = 128
SSD_CONV = 3
CHUNK = 128
D_BC = SSD_GROUPS * D_STATE
D_XBC = D_SSD + 2 * D_BC

D_CONF = 2048
CONF_KERNEL = 31

D_MIX = D_SSD + D_CONF
OFF_XBC = D_SSD
OFF_DT = OFF_XBC + D_XBC
OFF_CONF = OFF_DT + 2 * SSD_HEADS
N_IN = OFF_CONF + 2 * D_CONF

N_EXPERTS = 32
TOP_K = 4
D_FF = D_MODEL
SWIGLU_LIMIT = 7.0
SWIGLU_ALPHA = 1.702
MOE_BLOCK = 128

kernel_name = "hybrid_ssd_conformer_moe_diffusion_block"


def rmsnorm(x, g):
    xf = x.astype(jnp.float32)
    y = xf * lax.rsqrt(jnp.mean(xf * xf, axis=-1, keepdims=True) + EPS)
    return (y * g.astype(jnp.float32)).astype(x.dtype)


def modulate(h, shift, scale):
    return h * (1 + scale) + shift


def dwconv1d(x, w, b):
    k, ch = w.shape
    y = lax.conv_general_dilated(x, w[:, None, :].astype(x.dtype), window_strides=(1,),
                                 padding=[(k // 2, k // 2)], dimension_numbers=("NWC", "WIO", "NWC"),
                                 feature_group_count=ch)
    return y + b.astype(x.dtype)


def segsum(a):
    t = a.shape[-1]
    ae = jnp.broadcast_to(a[..., :, None], a.shape + (t,))
    ae = jnp.where(jnp.tril(jnp.ones((t, t), bool), -1), ae, 0.0)
    s = jnp.cumsum(ae, axis=-2)
    return jnp.where(jnp.tril(jnp.ones((t, t), bool)), s, -jnp.inf)


def ssd_chunked(x, a, bm, cm, init, want_y):
    bsz, t, _, p = x.shape
    nc = t // CHUNK
    x = x.reshape(bsz, nc, CHUNK, SSD_GROUPS, HEADS_PER_GROUP, p)
    a = a.reshape(bsz, nc, CHUNK, SSD_GROUPS, HEADS_PER_GROUP).transpose(0, 3, 4, 1, 2)
    bm = bm.reshape(bsz, nc, CHUNK, SSD_GROUPS, D_STATE)
    cm = cm.reshape(bsz, nc, CHUNK, SSD_GROUPS, D_STATE)
    a_cum = jnp.cumsum(a, axis=-1)
    decay_states = jnp.exp(a_cum[..., -1:] - a_cum)
    states = jnp.einsum("bclgn,bgrcl,bclgrp->bcgrpn", bm, decay_states, x)
    init = init.reshape(bsz, 1, SSD_GROUPS, HEADS_PER_GROUP, p, D_STATE)
    states = jnp.concatenate([init, states], axis=1)
    a_last = jnp.pad(a_cum[..., -1], ((0, 0), (0, 0), (0, 0), (1, 0)))
    chunk_decay = jnp.exp(segsum(a_last))
    new_states = jnp.einsum("bgrzc,bcgrpn->bzgrpn", chunk_decay, states)
    final = new_states[:, -1].reshape(bsz, SSD_HEADS, p, D_STATE)
    if not want_y:
        return None, final
    prev_states = new_states[:, :-1]
    lmat = jnp.exp(segsum(a))
    cb = jnp.einsum("bclgn,bcsgn->bgcls", cm, bm)
    y_diag = jnp.einsum("bgcls,bgrcls,bcsgrp->bclgrp", cb, lmat, x)
    y_off = jnp.einsum("bclgn,bcgrpn,bgrcl->bclgrp", cm, prev_states, jnp.exp(a_cum))
    return (y_diag + y_off).reshape(bsz, t, SSD_HEADS, p), final


def ssd_prepare(xbc_dt, conv_w, conv_b):
    bsz, t, _ = xbc_dt.shape
    xbc = jax.nn.silu(dwconv1d(xbc_dt[..., :D_XBC], conv_w, conv_b))
    xs = xbc[..., :D_SSD].reshape(bsz, t, SSD_HEADS, SSD_HEAD_DIM)
    bm = xbc[..., D_SSD:D_SSD + D_BC].reshape(bsz, t, SSD_GROUPS, D_STATE)
    cm = xbc[..., D_SSD + D_BC:].reshape(bsz, t, SSD_GROUPS, D_STATE)
    dt_f = xbc_dt[..., D_XBC:D_XBC + SSD_HEADS]
    dt_b = xbc_dt[..., D_XBC + SSD_HEADS:]
    return xs, bm, cm, dt_f, dt_b


def ssd_scan(xs, bm, cm, dt_raw, a_log, dt_bias, init, want_y):
    f32 = jnp.float32
    dt = jax.nn.softplus(dt_raw.astype(f32) + dt_bias.astype(f32))
    a = dt * (-jnp.exp(a_log.astype(f32)))
    return ssd_chunked(xs.astype(f32) * dt[..., None], a, bm.astype(f32), cm.astype(f32), init, want_y)


def ssd_bidir(xs, bm, cm, dt_f, dt_b, par_f, par_b, init_f, init_b, want_y):
    y_f, fin_f = ssd_scan(xs, bm, cm, dt_f, par_f[0], par_f[1], init_f, want_y)
    y_b, fin_b = ssd_scan(xs[:, ::-1], bm[:, ::-1], cm[:, ::-1], dt_b[:, ::-1], par_b[0], par_b[1], init_b, want_y)
    y = y_f + y_b[:, ::-1] if want_y else None
    return y, fin_f, fin_b


def ssd_output(y, xs, z, d_skip, norm_g):
    f32 = jnp.float32
    bsz, t = z.shape[:2]
    y = y + d_skip.astype(f32)[:, None] * xs.astype(f32)
    y = y.reshape(bsz, t, D_SSD) * jax.nn.silu(z.astype(f32))
    y = y.reshape(bsz, t, SSD_GROUPS, D_SSD // SSD_GROUPS)
    y = y * lax.rsqrt(jnp.mean(y * y, axis=-1, keepdims=True) + EPS)
    return (y.reshape(bsz, t, D_SSD) * norm_g.astype(f32)).astype(z.dtype)


def conformer_conv(u, dw_w, dw_b, ln_g, ln_b, rows):
    a, g = jnp.split(u, 2, axis=-1)
    v = a * jax.nn.sigmoid(g)
    bsz, t, ch = v.shape
    if rows is None:
        v = dwconv1d(v, dw_w, dw_b)
    else:
        half = ch // 2
        vg = v.reshape(bsz, rows, GRID_W, ch)
        v_row = dwconv1d(vg[..., :half].reshape(bsz * rows, GRID_W, half), dw_w[:, :half], dw_b[:half])
        v_row = v_row.reshape(bsz, rows, GRID_W, half)
        v_col = vg[..., half:].transpose(0, 2, 1, 3).reshape(bsz * GRID_W, rows, ch - half)
        v_col = dwconv1d(v_col, dw_w[:, half:], dw_b[half:])
        v_col = v_col.reshape(bsz, GRID_W, rows, ch - half).transpose(0, 2, 1, 3)
        v = jnp.concatenate([v_row, v_col], axis=-1).reshape(bsz, t, ch)
    vf = v.astype(jnp.float32)
    mu = jnp.mean(vf, axis=-1, keepdims=True)
    var = jnp.mean(jnp.square(vf - mu), axis=-1, keepdims=True)
    vf = (vf - mu) * lax.rsqrt(var + EPS) * ln_g.astype(jnp.float32) + ln_b.astype(jnp.float32)
    return jax.nn.silu(vf).astype(u.dtype)


def moe_ffn(h, w_router, b_router, w_gu, b_gu, w_down, b_down):
    n_tok = h.shape[0]
    logits = (h @ w_router + b_router).astype(jnp.float32)
    top_logit, top_idx = lax.top_k(logits, TOP_K)
    top_w = jax.nn.softmax(top_logit, axis=-1).astype(h.dtype)
    n_slot = n_tok * TOP_K
    e_flat = top_idx.reshape(-1)
    order = jnp.argsort(e_flat)
    e_sorted = e_flat[order]
    tok_sorted = (jnp.arange(n_slot, dtype=jnp.int32) // TOP_K)[order]
    w_sorted = top_w.reshape(-1)[order]
    counts = jnp.bincount(e_flat, length=N_EXPERTS)
    starts = jnp.cumsum(counts) - counts
    padded = (counts + MOE_BLOCK - 1) // MOE_BLOCK * MOE_BLOCK
    pad_ends = jnp.cumsum(padded)
    dest = (pad_ends - padded)[e_sorted] + jnp.arange(n_slot, dtype=jnp.int32) - starts[e_sorted]
    n_blocks = -(-n_slot // MOE_BLOCK) + N_EXPERTS
    n_pad = n_blocks * MOE_BLOCK
    slot_tok = jnp.zeros((n_pad,), jnp.int32).at[dest].set(tok_sorted)
    slot_w = jnp.zeros((n_pad,), h.dtype).at[dest].set(w_sorted)
    block_e = jnp.searchsorted(pad_ends, jnp.arange(n_blocks, dtype=jnp.int32) * MOE_BLOCK, side="right")
    block_e = jnp.minimum(block_e, N_EXPERTS - 1)

    def expert_block(args):
        e, tok, wt = args
        xb = h[tok]
        gu = xb @ w_gu[e] + b_gu[e]
        glu = jnp.minimum(gu[:, :D_FF], SWIGLU_LIMIT)
        lin = jnp.clip(gu[:, D_FF:], -SWIGLU_LIMIT, SWIGLU_LIMIT)
        act = glu * jax.nn.sigmoid(SWIGLU_ALPHA * glu) * (lin + 1)
        return (act @ w_down[e] + b_down[e]) * wt[:, None]

    out = lax.map(expert_block, (block_e, slot_tok.reshape(n_blocks, MOE_BLOCK), slot_w.reshape(n_blocks, MOE_BLOCK)))
    return jax.ops.segment_sum(out.reshape(n_pad, -1), slot_tok, num_segments=n_tok)


def setup_inputs(seed: int = 0) -> dict:
    key = jax.random.key(seed)
    ks = jax.random.split(key, 32)
    f32 = jnp.float32
    L = DEPTH

    def nrm(k, shape, s):
        return jax.random.normal(k, shape, f32) * s

    def gain(k, shape):
        return 1.0 + 0.01 * jax.random.normal(k, shape, f32)

    def dt_bias_init(k):
        dt0 = jnp.exp(jax.random.uniform(k, (L, SSD_HEADS), f32, math.log(1e-3), math.log(1e-1)))
        return dt0 + jnp.log(-jnp.expm1(-dt0))

    def a_log_init(k):
        return jnp.log(jax.random.uniform(k, (L, SSD_HEADS), f32, 1.0, 16.0))

    return {
        "x": nrm(ks[0], (BATCH, SEQ, D_MODEL), 1.0),
        "c": nrm(ks[1], (BATCH, D_MODEL), 1.0),
        "ctx": nrm(ks[2], (BATCH, CTX_LEN, D_MODEL), 1.0),
        "c_ctx": nrm(ks[3], (D_MODEL,), 1.0),
        "w_mod": nrm(ks[4], (L, D_MODEL, N_MOD * D_MODEL), 0.5 * D_MODEL ** -0.5),
        "b_mod": nrm(ks[5], (L, N_MOD * D_MODEL), 0.02),
        "norm_mix_g": gain(ks[6], (L, D_MODEL)),
        "norm_ffn_g": gain(ks[7], (L, D_MODEL)),
        "w_in": nrm(ks[8], (L, D_MODEL, N_IN), D_MODEL ** -0.5),
        "ssd_conv_w": nrm(ks[9], (L, SSD_CONV, D_XBC), SSD_CONV ** -0.5),
        "ssd_conv_b": nrm(ks[10], (L, D_XBC), 0.02),
        "ssd_a_log_f": a_log_init(ks[11]),
        "ssd_dt_bias_f": dt_bias_init(ks[12]),
        "ssd_a_log_b": a_log_init(ks[13]),
        "ssd_dt_bias_b": dt_bias_init(ks[14]),
        "ssd_d_skip": gain(ks[15], (L, SSD_HEADS)),
        "ssd_norm_g": gain(ks[16], (L, D_SSD)),
        "conf_dw_w": nrm(ks[17], (L, CONF_KERNEL, D_CONF), CONF_KERNEL ** -0.5),
        "conf_dw_b": nrm(ks[18], (L, D_CONF), 0.02),
        "conf_ln_g": gain(ks[19], (L, D_CONF)),
        "conf_ln_b": nrm(ks[20], (L, D_CONF), 0.02),
        "w_out": nrm(ks[21], (L, D_MIX, D_MODEL), D_MIX ** -0.5),
        "w_router": nrm(ks[22], (L, D_MODEL, N_EXPERTS), D_MODEL ** -0.5),
        "b_router": nrm(ks[23], (L, N_EXPERTS), 0.01),
        "w_gate_up": nrm(ks[24], (L, N_EXPERTS, D_MODEL, 2 * D_FF), D_MODEL ** -0.5),
        "b_gate_up": nrm(ks[25], (L, N_EXPERTS, 2 * D_FF), 0.02),
        "w_down": nrm(ks[26], (L, N_EXPERTS, D_FF, D_MODEL), D_FF ** -0.5),
        "b_down": nrm(ks[27], (L, N_EXPERTS, D_MODEL), 0.02),
        "final_norm_g": gain(ks[28], (D_MODEL,)),
    }


def reference(x, c, ctx, c_ctx, w_mod, b_mod, norm_mix_g, norm_ffn_g, w_in, ssd_conv_w, ssd_conv_b,
              ssd_a_log_f, ssd_dt_bias_f, ssd_a_log_b, ssd_dt_bias_b, ssd_d_skip, ssd_norm_g,
              conf_dw_w, conf_dw_b, conf_ln_g, conf_ln_b, w_out, w_router, b_router,
              w_gate_up, b_gate_up, w_down, b_down, final_norm_g):
    bsz, n_lat, _ = x.shape
    rows = n_lat // GRID_W
    xc = ctx
    for l in range(DEPTH):
        last = l == DEPTH - 1
        mod = jax.nn.silu(c) @ w_mod[l] + b_mod[l]
        sh_m, sc_m, g_m, sh_f, sc_f, g_f = jnp.split(mod[:, None, :], N_MOD, axis=-1)
        n_ctx_mod = 2 if last else N_MOD
        mod_c = jax.nn.silu(c_ctx) @ w_mod[l][:, :n_ctx_mod * D_MODEL] + b_mod[l][:n_ctx_mod * D_MODEL]
        mod_c = jnp.split(mod_c, n_ctx_mod)
        par_f = (ssd_a_log_f[l], ssd_dt_bias_f[l])
        par_b = (ssd_a_log_b[l], ssd_dt_bias_b[l])

        h = modulate(rmsnorm(x, norm_mix_g[l]), sh_m, sc_m)
        hc = modulate(rmsnorm(xc, norm_mix_g[l]), mod_c[0], mod_c[1])
        proj = h @ w_in[l]
        proj_c = hc @ (w_in[l][:, OFF_XBC:OFF_CONF] if last else w_in[l])
        ssd_in_c = proj_c if last else proj_c[..., OFF_XBC:OFF_CONF]

        xs_c, bm_c, cm_c, dtf_c, dtb_c = ssd_prepare(ssd_in_c, ssd_conv_w[l], ssd_conv_b[l])
        zero_state = jnp.zeros((bsz, SSD_HEADS, SSD_HEAD_DIM, D_STATE), jnp.float32)
        y_c, fin_f, fin_b = ssd_bidir(xs_c, bm_c, cm_c, dtf_c, dtb_c, par_f, par_b, zero_state, zero_state, not last)

        xs, bm, cm, dtf, dtb = ssd_prepare(proj[..., OFF_XBC:OFF_CONF], ssd_conv_w[l], ssd_conv_b[l])
        y, _, _ = ssd_bidir(xs, bm, cm, dtf, dtb, par_f, par_b, fin_f, fin_b, True)
        ssd_out = ssd_output(y, xs, proj[..., :OFF_XBC], ssd_d_skip[l], ssd_norm_g[l])
        conf_out = conformer_conv(proj[..., OFF_CONF:], conf_dw_w[l], conf_dw_b[l], conf_ln_g[l], conf_ln_b[l], rows)
        x = x + g_m * (jnp.concatenate([ssd_out, conf_out], axis=-1) @ w_out[l])

        h2 = modulate(rmsnorm(x, norm_ffn_g[l]), sh_f, sc_f)
        x = x + g_f * moe_ffn(h2.reshape(-1, D_MODEL), w_router[l], b_router[l], w_gate_up[l], b_gate_up[l],
                              w_down[l], b_down[l]).reshape(x.shape)

        if not last:
            ssd_out_c = ssd_output(y_c, xs_c, proj_c[..., :OFF_XBC], ssd_d_skip[l], ssd_norm_g[l])
            conf_out_c = conformer_conv(proj_c[..., OFF_CONF:], conf_dw_w[l], conf_dw_b[l], conf_ln_g[l], conf_ln_b[l], None)
            xc = xc + mod_c[2] * (jnp.concatenate([ssd_out_c, conf_out_c], axis=-1) @ w_out[l])
            h2c = modulate(rmsnorm(xc, norm_ffn_g[l]), mod_c[3], mod_c[4])
            xc = xc + mod_c[5] * moe_ffn(h2c.reshape(-1, D_MODEL), w_router[l], b_router[l], w_gate_up[l],
                                         b_gate_up[l], w_down[l], b_down[l]).reshape(xc.shape)
    return rmsnorm(x, final_norm_g)
```

```python
import functools

import jax
import jax.numpy as jnp
from jax import lax
from jax.experimental import pallas as pl
from jax.experimental.pallas import tpu as pltpu

F32 = jnp.float32
BF16 = jnp.bfloat16

D_MODEL = 2048
N_MOD = 6
EPS = 1e-6
GRID_W = 64

SSD_HEADS = 32
SSD_HEAD_DIM = 64
D_SSD = SSD_HEADS * SSD_HEAD_DIM
SSD_GROUPS = 4
HEADS_PER_GROUP = SSD_HEADS // SSD_GROUPS
D_GROUP = D_SSD // SSD_GROUPS
D_STATE = 128
CHUNK = 128
D_BC = SSD_GROUPS * D_STATE
D_CONF = 2048
CONF_KERNEL = 31
CONF_HALF = CONF_KERNEL // 2
N_EXPERTS = 32
TOP_K = 4
D_FF = D_MODEL
SWIGLU_LIMIT = 7.0
SWIGLU_ALPHA = 1.702

OFF_XBC = D_SSD
OFF_DT = OFF_XBC + D_SSD + 2 * D_BC
OFF_CONF = OFF_DT + 2 * SSD_HEADS

PK_Z = 0
PK_X = D_SSD
PK_B = PK_X + D_SSD
PK_C = PK_B + D_BC
PK_CA = PK_C + D_BC
PK_CG = PK_CA + D_CONF
PK_N = PK_CG + D_CONF

LANES = 128
VMEM_LIMIT_BYTES = 56 * 1024 * 1024
NEG_BIG = -1e30

PROJ_TN = 512
MOE_SUPER = 1024
MOE_SUB = 256
MOE_NSUB = MOE_SUPER // MOE_SUB
MOE_TF = 256
OUT_TM = 256
COMB_TM = 256
CONF_TC = LANES
CONF_ROWS = 128


def _cparams(sem):
    return pltpu.CompilerParams(dimension_semantics=sem, vmem_limit_bytes=VMEM_LIMIT_BYTES)


def _silu(v):
    return v * jax.nn.sigmoid(v)


def _dot(a, b):
    return jnp.dot(a, b, preferred_element_type=F32)


def _split2(v):
    hi = v.astype(BF16)
    lo = (v - hi.astype(F32)).astype(BF16)
    return hi, lo


def _split3(v):
    hi = v.astype(BF16)
    r1 = v - hi.astype(F32)
    mid = r1.astype(BF16)
    lo = (r1 - mid.astype(F32)).astype(BF16)
    return hi, mid, lo


def _mod_body(c_ref, w_ref, b_ref, o_ref):
    s = _silu(c_ref[...])
    o_ref[...] = _dot(s.astype(BF16), w_ref[...].astype(BF16)) + b_ref[...]


def _adaln_mod(c8, w_mod, b_mod):
    tn = 1024
    n = w_mod.shape[1]
    return pl.pallas_call(
        _mod_body,
        grid=(n // tn,),
        in_specs=[
            pl.BlockSpec((8, D_MODEL), lambda j: (0, 0)),
            pl.BlockSpec((D_MODEL, tn), lambda j: (0, j)),
            pl.BlockSpec((1, tn), lambda j: (0, j)),
        ],
        out_specs=pl.BlockSpec((8, tn), lambda j: (0, j)),
        out_shape=jax.ShapeDtypeStruct((8, n), F32),
        compiler_params=_cparams(("arbitrary",)),
        name="adaln_mod",
    )(c8, w_mod, b_mod)


def _inproj_body(x_ref, g_ref, sh_ref, sc_ref, w_ref, wdt_ref, o_ref, odt_ref, h_scr):
    @pl.when(pl.program_id(1) == 0)
    def _():
        xf = x_ref[...]
        ms = jnp.mean(xf * xf, axis=-1, keepdims=True)
        y = xf * lax.rsqrt(ms + EPS) * g_ref[...]
        hb = (y * (1.0 + sc_ref[0]) + sh_ref[0]).astype(BF16)
        h_scr[...] = hb
        odt_ref[...] = _dot(hb, wdt_ref[...])

    o_ref[...] = _dot(h_scr[...], w_ref[...]).astype(o_ref.dtype)


def _in_proj(x2d, gain, shift, scale, w_main, w_dt, col_block0, n_col_blocks, tm, rows_per_mod):
    m = x2d.shape[0]
    mods_per_tile = rows_per_mod // tm
    n_dt = w_dt.shape[1]
    return pl.pallas_call(
        _inproj_body,
        grid=(m // tm, n_col_blocks),
        in_specs=[
            pl.BlockSpec((tm, D_MODEL), lambda i, j: (i, 0)),
            pl.BlockSpec((1, D_MODEL), lambda i, j: (0, 0)),
            pl.BlockSpec((1, 1, D_MODEL), lambda i, j: (i // mods_per_tile, 0, 0)),
            pl.BlockSpec((1, 1, D_MODEL), lambda i, j: (i // mods_per_tile, 0, 0)),
            pl.BlockSpec((D_MODEL, PROJ_TN), lambda i, j: (0, col_block0 + j)),
            pl.BlockSpec((D_MODEL, n_dt), lambda i, j: (0, 0)),
        ],
        out_specs=[
            pl.BlockSpec((tm, PROJ_TN), lambda i, j: (i, j)),
            pl.BlockSpec((tm, n_dt), lambda i, j: (i, 0)),
        ],
        out_shape=[
            jax.ShapeDtypeStruct((m, n_col_blocks * PROJ_TN), BF16),
            jax.ShapeDtypeStruct((m, n_dt), F32),
        ],
        scratch_shapes=[pltpu.VMEM((tm, D_MODEL), BF16)],
        compiler_params=_cparams(("arbitrary", "arbitrary")),
        name="in_proj",
    )(x2d, gain, shift, scale, w_main, w_dt)


CONV_ROWS = 256
PAD = 8


def _conv3_silu(src_ref, pad_ref, w_ref, b_ref, dst_ref, n_rows):
    ncol = pad_ref.shape[1]
    zero = jnp.zeros((PAD, ncol), F32)
    pad_ref[0:PAD, :] = zero
    pad_ref[PAD + n_rows:PAD + n_rows + PAD, :] = zero
    win = CONV_ROWS + 2 * PAD

    def fill(i, c):
        s = pl.multiple_of(i * CONV_ROWS, CONV_ROWS)
        pad_ref[pl.ds(s + PAD, CONV_ROWS), :] = src_ref[0, pl.ds(s, CONV_ROWS), :].astype(F32)
        return c

    lax.fori_loop(0, n_rows // CONV_ROWS, fill, 0)

    def step(i, c):
        s = pl.multiple_of(i * CONV_ROWS, CONV_ROWS)
        w = pad_ref[pl.ds(s, win), :]
        prev = pltpu.roll(w, 1, 0)[PAD:PAD + CONV_ROWS]
        nxt = pltpu.roll(w, win - 1, 0)[PAD:PAD + CONV_ROWS]
        cur = w[PAD:PAD + CONV_ROWS]
        y = prev * w_ref[0:1, :] + cur * w_ref[1:2, :] + nxt * w_ref[2:3, :] + b_ref[...]
        dst_ref[pl.ds(s, CONV_ROWS), :] = _silu(y)
        return c

    lax.fori_loop(0, n_rows // CONV_ROWS, step, 0)


def _tri_dot(tri, v):
    hi, mid, lo = _split3(v)
    return _dot(tri, hi) + _dot(tri, mid) + _dot(tri, lo)


def _expand(v, e_mat):
    hi, lo = _split2(v)
    return _dot(hi, e_mat) + _dot(lo, e_mat)


def _ssd_dir_step(xs, bmat, cmat, cums, dts, st_ref, e_mat, is_fwd):
    edge = cums[CHUNK - 1:CHUNK, :] if is_fwd else cums[0:1, :]
    wdec = jnp.exp(edge - cums) * dts
    xw = (xs * _expand(wdec, e_mat)).astype(BF16)
    s_loc = _dot(bmat.T.astype(BF16), xw)
    tot = _expand(jnp.broadcast_to(jnp.exp(edge), (8, LANES)), e_mat)[0:1, :]
    s_prev = st_ref[...]
    y_off = None
    if cmat is not None:
        y_off = _dot(cmat.astype(BF16), s_prev.astype(BF16)) * _expand(jnp.exp(cums), e_mat)
    st_ref[...] = s_prev * tot + s_loc
    return y_off


def _ssd_diag(xs, cb, cum, rc, dt):
    cum_t, rc_t, dt_t = cum.T, rc.T, dt.T
    ii = lax.broadcasted_iota(jnp.int32, (CHUNK, CHUNK), 0)
    jj = lax.broadcasted_iota(jnp.int32, (CHUNK, CHUNK), 1)
    lower, upper = ii >= jj, jj >= ii
    first_half = jj < SSD_HEAD_DIM
    outs = []
    for pair in range(HEADS_PER_GROUP // 2):
        ms = []
        for h in (2 * pair, 2 * pair + 1):
            hb = HEADS_PER_GROUP + h
            lf = jnp.exp(jnp.where(lower, cum[:, h:h + 1] - cum_t[h:h + 1, :], NEG_BIG)) * dt_t[h:h + 1, :]
            lb = jnp.exp(jnp.where(upper, rc[:, hb:hb + 1] - rc_t[hb:hb + 1, :], NEG_BIG)) * dt_t[hb:hb + 1, :]
            ms.append((cb * (lf + lb)).astype(BF16))
        xp = xs[:, pair * LANES:(pair + 1) * LANES]
        rhs = jnp.concatenate([jnp.where(first_half, xp, 0.0), jnp.where(first_half, 0.0, xp)], axis=0)
        outs.append(_dot(jnp.concatenate(ms, axis=1), rhs.astype(BF16)))
    return jnp.concatenate(outs, axis=1)


def _ssd_body(xl_ref, bl_ref, cl_ref, zl_ref, dtl_ref, xc_ref, bc_ref, dtc_ref,
              cwx_ref, cwb_ref, cwc_ref, cbx_ref, cbb_ref, cbc_ref,
              dtbias_ref, alog_ref, dskip_ref, ng_ref,
              o_ref,
              xpad, bpad, xs_scr, bs_scr, cs_scr, a_scr, dt_scr, rc_scr,
              xcpad, bcpad, xsc_scr, bsc_scr, ac_scr, dtcs_scr, y_scr, st_scr):
    t_lat = xs_scr.shape[0]
    t_ctx = xsc_scr.shape[0]
    n_lat = t_lat // CHUNK
    n_ctx = t_ctx // CHUNK

    _conv3_silu(xl_ref, xpad, cwx_ref, cbx_ref, xs_scr, t_lat)
    _conv3_silu(bl_ref, bpad, cwb_ref, cbb_ref, bs_scr, t_lat)
    _conv3_silu(cl_ref, bpad, cwc_ref, cbc_ref, cs_scr, t_lat)
    _conv3_silu(xc_ref, xcpad, cwx_ref, cbx_ref, xsc_scr, t_ctx)
    _conv3_silu(bc_ref, bcpad, cwb_ref, cbb_ref, bsc_scr, t_ctx)
    neg_a = -jnp.exp(alog_ref[...])
    dt = jax.nn.softplus(dtl_ref[0] + dtbias_ref[...])
    dt_scr[...] = dt
    a_scr[...] = dt * neg_a
    dtc = jax.nn.softplus(dtc_ref[0] + dtbias_ref[...])
    dtcs_scr[...] = dtc
    ac_scr[...] = dtc * neg_a

    ii = lax.broadcasted_iota(jnp.int32, (CHUNK, CHUNK), 0)
    jj = lax.broadcasted_iota(jnp.int32, (CHUNK, CHUNK), 1)
    tril = (ii >= jj).astype(BF16)
    triu = (jj >= ii).astype(BF16)
    er = lax.broadcasted_iota(jnp.int32, (LANES, D_GROUP), 0)
    eh = lax.broadcasted_iota(jnp.int32, (LANES, D_GROUP), 1) // SSD_HEAD_DIM
    e_f = (er == eh).astype(BF16)
    e_b = (er == eh + HEADS_PER_GROUP).astype(BF16)

    def ctx_step(c, is_fwd):
        s = pl.multiple_of(c * CHUNK, CHUNK)
        cums = _tri_dot(tril if is_fwd else triu, ac_scr[pl.ds(s, CHUNK), :])
        _ssd_dir_step(xsc_scr[pl.ds(s, CHUNK), :], bsc_scr[pl.ds(s, CHUNK), :], None, cums,
                      dtcs_scr[pl.ds(s, CHUNK), :], st_scr, e_f if is_fwd else e_b, is_fwd)

    def lat_fwd(c, carry):
        s = pl.multiple_of(c * CHUNK, CHUNK)
        a_blk = a_scr[pl.ds(s, CHUNK), :]
        dt_blk = dt_scr[pl.ds(s, CHUNK), :]
        cum = _tri_dot(tril, a_blk)
        rc = _tri_dot(triu, a_blk)
        rc_scr[pl.ds(s, CHUNK), :] = rc
        xs = xs_scr[pl.ds(s, CHUNK), :]
        bm = bs_scr[pl.ds(s, CHUNK), :]
        cm = cs_scr[pl.ds(s, CHUNK), :]
        cb = lax.dot_general(cm.astype(BF16), bm.astype(BF16), (((1,), (1,)), ((), ())),
                             preferred_element_type=F32)
        y_diag = _ssd_diag(xs, cb, cum, rc, dt_blk)
        y_off = _ssd_dir_step(xs, bm, cm, cum, dt_blk, st_scr, e_f, True)
        y_scr[pl.ds(s, CHUNK), :] = y_diag + y_off
        return carry

    def lat_bwd(k, carry):
        s = pl.multiple_of((n_lat - 1 - k) * CHUNK, CHUNK)
        y_off = _ssd_dir_step(xs_scr[pl.ds(s, CHUNK), :], bs_scr[pl.ds(s, CHUNK), :], cs_scr[pl.ds(s, CHUNK), :],
                              rc_scr[pl.ds(s, CHUNK), :], dt_scr[pl.ds(s, CHUNK), :], st_scr, e_b, False)
        y_scr[pl.ds(s, CHUNK), :] += y_off
        return carry

    st_scr[...] = jnp.zeros(st_scr.shape, F32)
    for c in range(n_ctx):
        ctx_step(c, True)
    lax.fori_loop(0, n_lat, lat_fwd, 0)
    st_scr[...] = jnp.zeros(st_scr.shape, F32)
    for c in reversed(range(n_ctx)):
        ctx_step(c, False)
    lax.fori_loop(0, n_lat, lat_bwd, 0)

    def finish(i, carry):
        s = pl.multiple_of(i * CONV_ROWS, CONV_ROWS)
        y = y_scr[pl.ds(s, CONV_ROWS), :] + dskip_ref[...] * xs_scr[pl.ds(s, CONV_ROWS), :]
        y = y * _silu(zl_ref[0, pl.ds(s, CONV_ROWS), :].astype(F32))
        ms = jnp.mean(y * y, axis=-1, keepdims=True)
        o_ref[0, pl.ds(s, CONV_ROWS), :] = (y * lax.rsqrt(ms + EPS) * ng_ref[...]).astype(o_ref.dtype)
        return carry

    lax.fori_loop(0, t_lat // CONV_ROWS, finish, 0)


def _ssd(proj3, dt3, projc3, dtc3, conv_w, conv_b, dtbias_pk, alog_pk, dskip_row, norm_g_row):
    bsz, t_lat, _ = proj3.shape
    t_ctx = projc3.shape[1]
    xb, bb, cb_ = PK_X // D_GROUP, PK_B // D_STATE, PK_C // D_STATE
    cxb, cbb = 0, D_SSD // D_STATE
    wx0, wb0, wc0 = 0, D_SSD // D_STATE, (D_SSD + D_BC) // D_STATE
    in_specs = [
        pl.BlockSpec((1, t_lat, D_GROUP), lambda b, g: (b, 0, xb + g)),
        pl.BlockSpec((1, t_lat, D_STATE), lambda b, g: (b, 0, bb + g)),
        pl.BlockSpec((1, t_lat, D_STATE), lambda b, g: (b, 0, cb_ + g)),
        pl.BlockSpec((1, t_lat, D_GROUP), lambda b, g: (b, 0, g)),
        pl.BlockSpec((1, t_lat, LANES), lambda b, g: (b, 0, g)),
        pl.BlockSpec((1, t_ctx, D_GROUP), lambda b, g: (b, 0, cxb + g)),
        pl.BlockSpec((1, t_ctx, D_STATE), lambda b, g: (b, 0, cbb + g)),
        pl.BlockSpec((1, t_ctx, LANES), lambda b, g: (b, 0, g)),
        pl.BlockSpec((3, D_GROUP), lambda b, g: (0, wx0 + g)),
        pl.BlockSpec((3, D_STATE), lambda b, g: (0, wb0 + g)),
        pl.BlockSpec((3, D_STATE), lambda b, g: (0, wc0 + g)),
        pl.BlockSpec((1, D_GROUP), lambda b, g: (0, wx0 + g)),
        pl.BlockSpec((1, D_STATE), lambda b, g: (0, wb0 + g)),
        pl.BlockSpec((1, D_STATE), lambda b, g: (0, wc0 + g)),
        pl.BlockSpec((1, LANES), lambda b, g: (0, g)),
        pl.BlockSpec((1, LANES), lambda b, g: (0, g)),
        pl.BlockSpec((1, D_GROUP), lambda b, g: (0, g)),
        pl.BlockSpec((1, D_GROUP), lambda b, g: (0, g)),
    ]
    scratch = [
        pltpu.VMEM((t_lat + 2 * PAD, D_GROUP), F32),
        pltpu.VMEM((t_lat + 2 * PAD, D_STATE), F32),
        pltpu.VMEM((t_lat, D_GROUP), F32),
        pltpu.VMEM((t_lat, D_STATE), F32),
        pltpu.VMEM((t_lat, D_STATE), F32),
        pltpu.VMEM((t_lat, LANES), F32),
        pltpu.VMEM((t_lat, LANES), F32),
        pltpu.VMEM((t_lat, LANES), F32),
        pltpu.VMEM((t_ctx + 2 * PAD, D_GROUP), F32),
        pltpu.VMEM((t_ctx + 2 * PAD, D_STATE), F32),
        pltpu.VMEM((t_ctx, D_GROUP), F32),
        pltpu.VMEM((t_ctx, D_STATE), F32),
        pltpu.VMEM((t_ctx, LANES), F32),
        pltpu.VMEM((t_ctx, LANES), F32),
        pltpu.VMEM((t_lat, D_GROUP), F32),
        pltpu.VMEM((D_STATE, D_GROUP), F32),
    ]
    return pl.pallas_call(
        _ssd_body,
        grid=(bsz, SSD_GROUPS),
        in_specs=in_specs,
        out_specs=pl.BlockSpec((1, t_lat, D_GROUP), lambda b, g: (b, 0, g)),
        out_shape=jax.ShapeDtypeStruct((bsz, t_lat, D_SSD), BF16),
        scratch_shapes=scratch,
        compiler_params=_cparams(("arbitrary", "arbitrary")),
        name="ssd_bidir",
    )(proj3, proj3, proj3, proj3, dt3, projc3, projc3, dtc3,
      conv_w, conv_w, conv_w, conv_b, conv_b, conv_b, dtbias_pk, alog_pk, dskip_row, norm_g_row)


def _conf_body(a_ref, g_ref, w_ref, b_ref, o_ref, v_scr, p_scr, c_scr, *, along_w):
    t = v_scr.shape[0]
    rows = t // GRID_W
    minor = rows if along_w else GRID_W
    halo = CONF_HALF * minor
    tc = v_scr.shape[1]

    def glu(i, c):
        s = pl.multiple_of(i * CONV_ROWS, CONV_ROWS)
        a = a_ref[0, pl.ds(s, CONV_ROWS), :].astype(F32)
        g = g_ref[0, pl.ds(s, CONV_ROWS), :].astype(F32)
        dst = v_scr if along_w else p_scr
        off = 0 if along_w else halo
        dst[pl.ds(s + off, CONV_ROWS), :] = a * jax.nn.sigmoid(g)
        return c

    lax.fori_loop(0, t // CONV_ROWS, glu, 0)
    p_scr[0:halo, :] = jnp.zeros((halo, tc), F32)
    p_scr[halo + t:halo + t + halo, :] = jnp.zeros((halo, tc), F32)
    if along_w:
        for w in range(GRID_W):
            p_scr[halo + w * rows:halo + (w + 1) * rows, :] = v_scr[pl.ds(w, rows, stride=GRID_W), :]

    def conv(i, c):
        s = pl.multiple_of(i * CONF_ROWS, CONF_ROWS)
        acc = jnp.broadcast_to(b_ref[...], (CONF_ROWS, tc))
        for k in range(CONF_KERNEL):
            acc = acc + w_ref[k:k + 1, :] * p_scr[pl.ds(s + k * minor, CONF_ROWS), :]
        if along_w:
            c_scr[pl.ds(s, CONF_ROWS), :] = acc
        else:
            o_ref[0, pl.ds(s, CONF_ROWS), :] = acc.astype(o_ref.dtype)
        return c

    lax.fori_loop(0, t // CONF_ROWS, conv, 0)
    if along_w:
        for r in range(rows):
            o_ref[0, r * GRID_W:(r + 1) * GRID_W, :] = c_scr[pl.ds(r, GRID_W, stride=rows), :].astype(o_ref.dtype)


def _conf_conv(proj3, dw_w, dw_b_row, along_w):
    bsz, t, _ = proj3.shape
    rows = t // GRID_W
    half = D_CONF // 2
    c0 = 0 if along_w else half
    a0, g0, w0 = (PK_CA + c0) // CONF_TC, (PK_CG + c0) // CONF_TC, c0 // CONF_TC
    halo = CONF_HALF * (rows if along_w else GRID_W)
    return pl.pallas_call(
        functools.partial(_conf_body, along_w=along_w),
        grid=(bsz, half // CONF_TC),
        in_specs=[
            pl.BlockSpec((1, t, CONF_TC), lambda b, j: (b, 0, a0 + j)),
            pl.BlockSpec((1, t, CONF_TC), lambda b, j: (b, 0, g0 + j)),
            pl.BlockSpec((CONF_KERNEL, CONF_TC), lambda b, j: (0, w0 + j)),
            pl.BlockSpec((1, CONF_TC), lambda b, j: (0, w0 + j)),
        ],
        out_specs=pl.BlockSpec((1, t, CONF_TC), lambda b, j: (b, 0, j)),
        out_shape=jax.ShapeDtypeStruct((bsz, t, half), BF16),
        scratch_shapes=[
            pltpu.VMEM((t, CONF_TC), F32),
            pltpu.VMEM((t + 2 * halo, CONF_TC), F32),
            pltpu.VMEM((t, CONF_TC), F32),
        ],
        compiler_params=_cparams(("arbitrary", "arbitrary")),
        name="conf_conv_w" if along_w else "conf_conv_r",
    )(proj3, proj3, dw_w, dw_b_row)


def _outproj_body(ssd_ref, cw_ref, cr_ref, x_ref, wt_ref, wb_ref, lng_ref, lnb_ref, gm_ref,
                  ng_ref, shf_ref, scf_ref, wrh_ref, wrl_ref, br_ref,
                  x1_ref, h2_ref, route_ref, cnt_ref, cnt_scr):
    i = pl.program_id(0)
    tm = x_ref.shape[0]

    @pl.when(i == 0)
    def _():
        cnt_scr[...] = jnp.zeros(cnt_scr.shape, F32)

    v = jnp.concatenate([cw_ref[...].astype(F32), cr_ref[...].astype(F32)], axis=1)
    mu = jnp.mean(v, axis=-1, keepdims=True)
    vc = v - mu
    var = jnp.mean(vc * vc, axis=-1, keepdims=True)
    cf = _silu(vc * lax.rsqrt(var + EPS) * lng_ref[...] + lnb_ref[...])
    acc = _dot(ssd_ref[...], wt_ref[...]) + _dot(cf.astype(BF16), wb_ref[...])
    x1 = x_ref[...] + gm_ref[0] * acc
    x1_ref[...] = x1

    ms = jnp.mean(x1 * x1, axis=-1, keepdims=True)
    h2 = x1 * lax.rsqrt(ms + EPS) * ng_ref[...] * (1.0 + scf_ref[0]) + shf_ref[0]
    h2_ref[...] = h2

    hh, hl = _split2(h2)
    logit = _dot(hh, wrh_ref[...]) + _dot(hl, wrh_ref[...]) + _dot(hh, wrl_ref[...]) + br_ref[...]
    lane = lax.broadcasted_iota(jnp.int32, (tm, LANES), 1).astype(F32)
    tops, hots, idxs = [], [], []
    for _ in range(TOP_K):
        m = jnp.max(logit, axis=-1, keepdims=True)
        idx = jnp.min(jnp.where(logit == m, lane, float(LANES)), axis=-1, keepdims=True)
        hot = lane == idx
        tops.append(m)
        hots.append(hot)
        idxs.append(idx)
        logit = jnp.where(hot, -jnp.inf, logit)
    es = [jnp.exp(m - tops[0]) for m in tops]
    den = es[0] + es[1] + es[2] + es[3]

    sel = jnp.zeros((tm, LANES), F32)
    for hot in hots:
        sel = sel + hot.astype(F32)
    ii = lax.broadcasted_iota(jnp.int32, (tm, tm), 0)
    jj = lax.broadcasted_iota(jnp.int32, (tm, tm), 1)
    before = _dot((ii > jj).astype(BF16), sel.astype(BF16)) + cnt_scr[0:1, :]
    packed = jnp.zeros((tm, LANES), F32)
    for k in range(TOP_K):
        pos = jnp.sum(jnp.where(hots[k], before, 0.0), axis=-1, keepdims=True)
        packed = packed + jnp.where(lane == float(k), idxs[k], 0.0)
        packed = packed + jnp.where(lane == float(TOP_K + k), es[k] / den, 0.0)
        packed = packed + jnp.where(lane == float(2 * TOP_K + k), pos, 0.0)
    route_ref[...] = packed
    cnt_scr[...] = cnt_scr[...] + jnp.sum(sel, axis=0, keepdims=True)
    cnt_ref[...] = cnt_scr[...]


def _out_proj(ssd2, confw2, confr2, x2, w_out_bf, ln_g, ln_b, g_m, ffn_g, sh_f, sc_f, wr_hi, wr_lo, br_pad,
              rows_per_mod):
    m = x2.shape[0]
    tm = OUT_TM
    per = rows_per_mod // tm
    half = D_CONF // 2
    row = lambda i: (i, 0)
    fixed = lambda i: (0, 0)
    mod = lambda i: (i // per, 0, 0)
    return pl.pallas_call(
        _outproj_body,
        grid=(m // tm,),
        in_specs=[
            pl.BlockSpec((tm, D_SSD), row),
            pl.BlockSpec((tm, half), row),
            pl.BlockSpec((tm, half), row),
            pl.BlockSpec((tm, D_MODEL), row),
            pl.BlockSpec((D_SSD, D_MODEL), lambda i: (0, 0), pipeline_mode=pl.Buffered(1)),
            pl.BlockSpec((D_CONF, D_MODEL), lambda i: (1, 0), pipeline_mode=pl.Buffered(1)),
            pl.BlockSpec((1, D_CONF), fixed),
            pl.BlockSpec((1, D_CONF), fixed),
            pl.BlockSpec((1, 1, D_MODEL), mod),
            pl.BlockSpec((1, D_MODEL), fixed),
            pl.BlockSpec((1, 1, D_MODEL), mod),
            pl.BlockSpec((1, 1, D_MODEL), mod),
            pl.BlockSpec((D_MODEL, LANES), fixed),
            pl.BlockSpec((D_MODEL, LANES), fixed),
            pl.BlockSpec((1, LANES), fixed),
        ],
        out_specs=[
            pl.BlockSpec((tm, D_MODEL), row),
            pl.BlockSpec((tm, D_MODEL), row),
            pl.BlockSpec((tm, LANES), row),
            pl.BlockSpec((8, LANES), fixed),
        ],
        out_shape=[
            jax.ShapeDtypeStruct((m, D_MODEL), F32),
            jax.ShapeDtypeStruct((m, D_MODEL), F32),
            jax.ShapeDtypeStruct((m, LANES), F32),
            jax.ShapeDtypeStruct((8, LANES), F32),
        ],
        scratch_shapes=[pltpu.VMEM((8, LANES), F32)],
        compiler_params=_cparams(("arbitrary",)),
        name="out_proj_route",
    )(ssd2, confw2, confr2, x2, w_out_bf, w_out_bf, ln_g, ln_b, g_m, ffn_g, sh_f, sc_f, wr_hi, wr_lo, br_pad)


def _gather_body(nsub_ref, tok_ref, src_ref, dst_ref, sem):
    i = pl.program_id(0)

    @pl.when(nsub_ref[i] > 0)
    def _():
        base = i * MOE_SUPER

        def issue(r, c):
            pltpu.make_async_copy(src_ref.at[pl.ds(tok_ref[0, 0, r], 1)], dst_ref.at[pl.ds(base + r, 1)], sem).start()
            return c

        lax.fori_loop(0, MOE_SUPER, issue, 0)

        def drain(r, c):
            pltpu.make_async_copy(src_ref.at[pl.ds(0, 1)], dst_ref.at[pl.ds(base, 1)], sem).wait()
            return c

        lax.fori_loop(0, MOE_SUPER, drain, 0)


def _dispatch(h2p, slot_tok3, nsub, n_super):
    width = h2p.shape[1]
    return pl.pallas_call(
        _gather_body,
        grid_spec=pltpu.PrefetchScalarGridSpec(
            num_scalar_prefetch=1,
            grid=(n_super,),
            in_specs=[
                pl.BlockSpec((1, 1, MOE_SUPER), lambda i, ns: (i, 0, 0), memory_space=pltpu.SMEM),
                pl.BlockSpec(memory_space=pl.ANY),
            ],
            out_specs=pl.BlockSpec(memory_space=pl.ANY),
            scratch_shapes=[pltpu.SemaphoreType.DMA(())],
        ),
        out_shape=jax.ShapeDtypeStruct((n_super * MOE_SUPER, width), h2p.dtype),
        compiler_params=_cparams(("arbitrary",)),
        name="moe_dispatch",
    )(nsub, slot_tok3, h2p)


def _moe_body(be_ref, nsub_ref, bi_ref, x_ref, wg_ref, wl_ref, bg_ref, bl_ref, wd_ref, bd_ref, o_ref,
              wg_scr, wl_scr, wd_scr):
    i = pl.program_id(0)
    f = pl.program_id(1)
    ns = nsub_ref[i]

    @pl.when(ns > 0)
    def _():
        @pl.when(f == 0)
        def _():
            o_ref[...] = jnp.broadcast_to(bd_ref[0], o_ref.shape)

        wg_scr[...] = wg_ref[0].astype(BF16)
        wl_scr[...] = wl_ref[0].astype(BF16)
        wd_scr[...] = wd_ref[0].astype(BF16)

        def sub(s, c):
            r0 = pl.multiple_of(s * MOE_SUB, MOE_SUB)
            xb = x_ref[pl.ds(r0, MOE_SUB), :].astype(BF16)
            gate = _dot(xb, wg_scr[...]) + bg_ref[0]
            lin = _dot(xb, wl_scr[...]) + bl_ref[0]
            glu = jnp.minimum(gate, SWIGLU_LIMIT)
            lin = jnp.clip(lin, -SWIGLU_LIMIT, SWIGLU_LIMIT)
            act = glu * jax.nn.sigmoid(SWIGLU_ALPHA * glu) * (lin + 1.0)
            o_ref[pl.ds(r0, MOE_SUB), :] += _dot(act.astype(BF16), wd_scr[...])
            return c

        lax.fori_loop(0, ns, sub, 0)


def _moe_mlp(xs, block_e, nsub, block_idx, w_gate_up, b_gate_up3, w_down, b_down3, n_super):
    n_ff = D_FF // MOE_TF
    lin0 = D_FF // MOE_TF

    def ff(i, f, ns):
        return jnp.where(ns[i] > 0, f, n_ff - 1)

    return pl.pallas_call(
        _moe_body,
        grid_spec=pltpu.PrefetchScalarGridSpec(
            num_scalar_prefetch=3,
            grid=(n_super, n_ff),
            in_specs=[
                pl.BlockSpec((MOE_SUPER, D_MODEL), lambda i, f, be, ns, bi: (bi[i], 0)),
                pl.BlockSpec((1, D_MODEL, MOE_TF), lambda i, f, be, ns, bi: (be[i], 0, ff(i, f, ns))),
                pl.BlockSpec((1, D_MODEL, MOE_TF), lambda i, f, be, ns, bi: (be[i], 0, lin0 + ff(i, f, ns))),
                pl.BlockSpec((1, 1, MOE_TF), lambda i, f, be, ns, bi: (be[i], 0, ff(i, f, ns))),
                pl.BlockSpec((1, 1, MOE_TF), lambda i, f, be, ns, bi: (be[i], 0, lin0 + ff(i, f, ns))),
                pl.BlockSpec((1, MOE_TF, D_MODEL), lambda i, f, be, ns, bi: (be[i], ff(i, f, ns), 0)),
                pl.BlockSpec((1, 1, D_MODEL), lambda i, f, be, ns, bi: (be[i], 0, 0)),
            ],
            out_specs=pl.BlockSpec((MOE_SUPER, D_MODEL), lambda i, f, be, ns, bi: (bi[i], 0)),
            scratch_shapes=[
                pltpu.VMEM((D_MODEL, MOE_TF), BF16),
                pltpu.VMEM((D_MODEL, MOE_TF), BF16),
                pltpu.VMEM((MOE_TF, D_MODEL), BF16),
            ],
        ),
        out_shape=jax.ShapeDtypeStruct((n_super * MOE_SUPER, D_MODEL), F32),
        compiler_params=_cparams(("arbitrary", "arbitrary")),
        name="moe_mlp",
    )(block_e, nsub, block_idx, xs, w_gate_up, w_gate_up, b_gate_up3, b_gate_up3, w_down, b_down3)


def _combine_body(dest_ref, y_ref, x1_ref, route_ref, gf_ref, fg_ref, o_ref, buf, sem):
    tm = x1_ref.shape[0]

    def issue(r, c):
        for k in range(TOP_K):
            pltpu.make_async_copy(y_ref.at[pl.ds(dest_ref[0, 0, r * TOP_K + k], 1)], buf.at[k, pl.ds(r, 1)], sem).start()
        return c

    lax.fori_loop(0, tm, issue, 0)

    def drain(r, c):
        for k in range(TOP_K):
            pltpu.make_async_copy(y_ref.at[pl.ds(0, 1)], buf.at[k, pl.ds(0, 1)], sem).wait()
        return c

    lax.fori_loop(0, tm, drain, 0)

    route = route_ref[...]
    moe = jnp.zeros(x1_ref.shape, F32)
    for k in range(TOP_K):
        moe = moe + route[:, TOP_K + k:TOP_K + k + 1] * buf[k]
    x2 = x1_ref[...] + gf_ref[0] * moe
    ms = jnp.mean(x2 * x2, axis=-1, keepdims=True)
    o_ref[...] = x2 * lax.rsqrt(ms + EPS) * fg_ref[...]


def _combine(dest3, y, x1, route, g_f, final_g, rows_per_mod):
    m = x1.shape[0]
    tm = COMB_TM
    per = rows_per_mod // tm
    return pl.pallas_call(
        _combine_body,
        grid=(m // tm,),
        in_specs=[
            pl.BlockSpec((1, 1, tm * TOP_K), lambda i: (i, 0, 0), memory_space=pltpu.SMEM),
            pl.BlockSpec(memory_space=pl.ANY),
            pl.BlockSpec((tm, D_MODEL), lambda i: (i, 0)),
            pl.BlockSpec((tm, LANES), lambda i: (i, 0)),
            pl.BlockSpec((1, 1, D_MODEL), lambda i: (i // per, 0, 0)),
            pl.BlockSpec((1, D_MODEL), lambda i: (0, 0)),
        ],
        out_specs=pl.BlockSpec((tm, D_MODEL), lambda i: (i, 0)),
        out_shape=jax.ShapeDtypeStruct((m, D_MODEL), F32),
        scratch_shapes=[pltpu.VMEM((TOP_K, tm, D_MODEL), F32), pltpu.SemaphoreType.DMA(())],
        compiler_params=_cparams(("arbitrary",)),
        name="moe_combine_norm",
    )(dest3, y, x1, route, g_f, final_g)


def _pack_heads(fwd, bwd):
    f = fwd.reshape(SSD_GROUPS, HEADS_PER_GROUP)
    b = bwd.reshape(SSD_GROUPS, HEADS_PER_GROUP)
    z = jnp.zeros((SSD_GROUPS, LANES - 2 * HEADS_PER_GROUP), F32)
    return jnp.concatenate([f, b, z], axis=1).reshape(1, SSD_GROUPS * LANES)


def _routing_tables(route, cnt, n_tok, n_super):
    idx = route[:, 0:TOP_K].astype(jnp.int32)
    pos = route[:, 2 * TOP_K:3 * TOP_K].astype(jnp.int32)
    counts = cnt[0, :N_EXPERTS].astype(jnp.int32)
    padded = (counts + MOE_SUPER - 1) // MOE_SUPER * MOE_SUPER
    pad_end = jnp.cumsum(padded)
    pad_start = pad_end - padded
    dest = pad_start[idx] + pos
    n_live = pad_end[-1] // MOE_SUPER
    block_idx = jnp.minimum(jnp.arange(n_super, dtype=jnp.int32), n_live - 1)
    blk_start = block_idx * MOE_SUPER
    block_e = jnp.minimum(jnp.searchsorted(pad_end, blk_start, side="right"), N_EXPERTS - 1).astype(jnp.int32)
    live_rows = jnp.clip(counts[block_e] - (blk_start - pad_start[block_e]), 0, MOE_SUPER)
    nsub = (live_rows + MOE_SUB - 1) // MOE_SUB
    nsub = jnp.where(jnp.arange(n_super) < n_live, nsub, 0).astype(jnp.int32)
    tok = jnp.broadcast_to(jnp.arange(n_tok, dtype=jnp.int32)[:, None], (n_tok, TOP_K))
    slot_tok = jnp.zeros((n_super * MOE_SUPER,), jnp.int32).at[dest.reshape(-1)].set(tok.reshape(-1))
    return dest, block_e, nsub, block_idx, slot_tok


def kernel(x, c, ctx, c_ctx, w_mod, b_mod, norm_mix_g, norm_ffn_g, w_in, ssd_conv_w, ssd_conv_b, ssd_a_log_f, ssd_dt_bias_f, ssd_a_log_b, ssd_dt_bias_b, ssd_d_skip, ssd_norm_g, conf_dw_w, conf_dw_b, conf_ln_g, conf_ln_b, w_out, w_router, b_router, w_gate_up, b_gate_up, w_down, b_down, final_norm_g):
    bsz, n_lat, _ = x.shape
    n_ctx = ctx.shape[1]
    n_tok = bsz * n_lat
    l = 0

    c8 = jnp.zeros((8, D_MODEL), F32).at[:bsz].set(c).at[bsz].set(c_ctx)
    mod = _adaln_mod(c8, w_mod[l], b_mod[l][None, :])
    sh_m, sc_m, g_m, sh_f, sc_f, g_f = [mod[:bsz, k * D_MODEL:(k + 1) * D_MODEL].reshape(bsz, 1, D_MODEL)
                                        for k in range(N_MOD)]
    shc = mod[bsz, 0:D_MODEL].reshape(1, 1, D_MODEL)
    scc = mod[bsz, D_MODEL:2 * D_MODEL].reshape(1, 1, D_MODEL)

    wi = w_in[l]
    w_main = jnp.concatenate([wi[:, :OFF_DT], wi[:, OFF_CONF:]], axis=1).astype(BF16)
    wf = wi[:, OFF_DT:OFF_DT + SSD_HEADS].reshape(D_MODEL, SSD_GROUPS, HEADS_PER_GROUP)
    wb = wi[:, OFF_DT + SSD_HEADS:OFF_CONF].reshape(D_MODEL, SSD_GROUPS, HEADS_PER_GROUP)
    wz = jnp.zeros((D_MODEL, SSD_GROUPS, LANES - 2 * HEADS_PER_GROUP), F32)
    w_dt = jnp.concatenate([wf, wb, wz], axis=2).reshape(D_MODEL, SSD_GROUPS * LANES).astype(BF16)

    gain_mix = norm_mix_g[l][None, :]
    proj, dt_raw = _in_proj(x.reshape(n_tok, D_MODEL), gain_mix, sh_m, sc_m, w_main, w_dt,
                            0, PK_N // PROJ_TN, 1024, n_lat)
    n_cb = (D_SSD + 2 * D_BC) // PROJ_TN
    projc, dtc_raw = _in_proj(ctx.reshape(bsz * n_ctx, D_MODEL), gain_mix, shc, scc, w_main, w_dt,
                              PK_X // PROJ_TN, n_cb, bsz * n_ctx, bsz * n_ctx)
    proj3 = proj.reshape(bsz, n_lat, PK_N)

    ssd_out = _ssd(proj3, dt_raw.reshape(bsz, n_lat, -1), projc.reshape(bsz, n_ctx, -1),
                   dtc_raw.reshape(bsz, n_ctx, -1), ssd_conv_w[l], ssd_conv_b[l][None, :],
                   _pack_heads(ssd_dt_bias_f[l], ssd_dt_bias_b[l]), _pack_heads(ssd_a_log_f[l], ssd_a_log_b[l]),
                   jnp.repeat(ssd_d_skip[l], SSD_HEAD_DIM)[None, :], ssd_norm_g[l][None, :])

    conf_w = _conf_conv(proj3, conf_dw_w[l], conf_dw_b[l][None, :], True)
    conf_r = _conf_conv(proj3, conf_dw_w[l], conf_dw_b[l][None, :], False)

    wr = jnp.zeros((D_MODEL, LANES), F32).at[:, :N_EXPERTS].set(w_router[l])
    wr_hi = wr.astype(BF16)
    wr_lo = (wr - wr_hi.astype(F32)).astype(BF16)
    br = jnp.full((1, LANES), NEG_BIG, F32).at[0, :N_EXPERTS].set(b_router[l])
    half = D_CONF // 2
    x1, h2p, route, cnt = _out_proj(
        ssd_out.reshape(n_tok, D_SSD), conf_w.reshape(n_tok, half), conf_r.reshape(n_tok, half),
        x.reshape(n_tok, D_MODEL), w_out[l].astype(BF16), conf_ln_g[l][None, :], conf_ln_b[l][None, :], g_m,
        norm_ffn_g[l][None, :], sh_f, sc_f, wr_hi, wr_lo, br, n_lat)

    n_super = n_tok * TOP_K // MOE_SUPER + N_EXPERTS
    dest, block_e, nsub, block_idx, slot_tok = _routing_tables(route, cnt, n_tok, n_super)
    xs = _dispatch(h2p, slot_tok.reshape(n_super, 1, MOE_SUPER), nsub, n_super)
    y = _moe_mlp(xs, block_e, nsub, block_idx, w_gate_up[l], b_gate_up[l][:, None, :], w_down[l], b_down[l][:, None, :], n_super)
    out = _combine(dest.reshape(n_tok // COMB_TM, 1, COMB_TM * TOP_K), y, x1, route, g_f,
                   final_norm_g[None, :], n_lat)
    return out.reshape(bsz, n_lat, D_MODEL)
```

```python
import functools

import jax
import jax.numpy as jnp
from jax import lax
from jax.experimental import pallas as pl
from jax.experimental.pallas import tpu as pltpu

F32 = jnp.float32
BF16 = jnp.bfloat16

D_MODEL = 2048
N_MOD = 6
EPS = 1e-6
GRID_W = 64

SSD_HEADS = 32
SSD_HEAD_DIM = 64
D_SSD = SSD_HEADS * SSD_HEAD_DIM
SSD_GROUPS = 4
HEADS_PER_GROUP = SSD_HEADS // SSD_GROUPS
D_GROUP = D_SSD // SSD_GROUPS
D_STATE = 128
CHUNK = 128
D_BC = SSD_GROUPS * D_STATE
D_CONF = 2048
CONF_KERNEL = 31
CONF_HALF = CONF_KERNEL // 2
N_EXPERTS = 32
TOP_K = 4
D_FF = D_MODEL
SWIGLU_LIMIT = 7.0
SWIGLU_ALPHA = 1.702

OFF_XBC = D_SSD
OFF_DT = OFF_XBC + D_SSD + 2 * D_BC
OFF_CONF = OFF_DT + 2 * SSD_HEADS

PK_Z = 0
PK_X = D_SSD
PK_B = PK_X + D_SSD
PK_C = PK_B + D_BC
PK_CA = PK_C + D_BC
PK_CG = PK_CA + D_CONF
PK_N = PK_CG + D_CONF

LANES = 128
VMEM_LIMIT_BYTES = 56 * 1024 * 1024
NEG_BIG = -1e30

PROJ_TN = 512
MOE_SUPER = 1024
MOE_SUB = 256
MOE_NSUB = MOE_SUPER // MOE_SUB
MOE_TF = 256
OUT_TM = 256
COMB_TM = 256
CONF_TC = LANES
CONF_ROWS = 128


def _cparams(sem):
    return pltpu.CompilerParams(dimension_semantics=sem, vmem_limit_bytes=VMEM_LIMIT_BYTES)


def _silu(v):
    return v * jax.nn.sigmoid(v)


def _dot(a, b):
    return jnp.dot(a, b, preferred_element_type=F32)


def _split2(v):
    hi = v.astype(BF16)
    lo = (v - hi.astype(F32)).astype(BF16)
    return hi, lo


def _split3(v):
    hi = v.astype(BF16)
    r1 = v - hi.astype(F32)
    mid = r1.astype(BF16)
    lo = (r1 - mid.astype(F32)).astype(BF16)
    return hi, mid, lo


def _mod_body(c_ref, w_ref, b_ref, o_ref):
    s = _silu(c_ref[...])
    o_ref[...] = _dot(s.astype(BF16), w_ref[...].astype(BF16)) + b_ref[...]


def _adaln_mod(c8, w_mod, b_mod):
    tn = 1024
    n = w_mod.shape[1]
    return pl.pallas_call(
        _mod_body,
        grid=(n // tn,),
        in_specs=[
            pl.BlockSpec((8, D_MODEL), lambda j: (0, 0)),
            pl.BlockSpec((D_MODEL, tn), lambda j: (0, j)),
            pl.BlockSpec((1, tn), lambda j: (0, j)),
        ],
        out_specs=pl.BlockSpec((8, tn), lambda j: (0, j)),
        out_shape=jax.ShapeDtypeStruct((8, n), F32),
        compiler_params=_cparams(("arbitrary",)),
        name="adaln_mod",
    )(c8, w_mod, b_mod)


def _inproj_body(x_ref, g_ref, sh_ref, sc_ref, w_ref, wdt_ref, o_ref, odt_ref, h_scr):
    @pl.when(pl.program_id(1) == 0)
    def _():
        xf = x_ref[...]
        ms = jnp.mean(xf * xf, axis=-1, keepdims=True)
        y = xf * lax.rsqrt(ms + EPS) * g_ref[...]
        hb = (y * (1.0 + sc_ref[0]) + sh_ref[0]).astype(BF16)
        h_scr[...] = hb
        odt_ref[...] = _dot(hb, wdt_ref[...])

    o_ref[...] = _dot(h_scr[...], w_ref[...]).astype(o_ref.dtype)


def _in_proj(x2d, gain, shift, scale, w_main, w_dt, col_block0, n_col_blocks, tm, rows_per_mod):
    m = x2d.shape[0]
    mods_per_tile = rows_per_mod // tm
    n_dt = w_dt.shape[1]
    return pl.pallas_call(
        _inproj_body,
        grid=(m // tm, n_col_blocks),
        in_specs=[
            pl.BlockSpec((tm, D_MODEL), lambda i, j: (i, 0)),
            pl.BlockSpec((1, D_MODEL), lambda i, j: (0, 0)),
            pl.BlockSpec((1, 1, D_MODEL), lambda i, j: (i // mods_per_tile, 0, 0)),
            pl.BlockSpec((1, 1, D_MODEL), lambda i, j: (i // mods_per_tile, 0, 0)),
            pl.BlockSpec((D_MODEL, PROJ_TN), lambda i, j: (0, col_block0 + j)),
            pl.BlockSpec((D_MODEL, n_dt), lambda i, j: (0, 0)),
        ],
        out_specs=[
            pl.BlockSpec((tm, PROJ_TN), lambda i, j: (i, j)),
            pl.BlockSpec((tm, n_dt), lambda i, j: (i, 0)),
        ],
        out_shape=[
            jax.ShapeDtypeStruct((m, n_col_blocks * PROJ_TN), BF16),
            jax.ShapeDtypeStruct((m, n_dt), F32),
        ],
        scratch_shapes=[pltpu.VMEM((tm, D_MODEL), BF16)],
        compiler_params=_cparams(("arbitrary", "arbitrary")),
        name="in_proj",
    )(x2d, gain, shift, scale, w_main, w_dt)


CONV_ROWS = 256
PAD = 8


def _conv3_silu(src_ref, pad_ref, w_ref, b_ref, dst_ref, n_rows):
    ncol = pad_ref.shape[1]
    zero = jnp.zeros((PAD, ncol), F32)
    pad_ref[0:PAD, :] = zero
    pad_ref[PAD + n_rows:PAD + n_rows + PAD, :] = zero
    win = CONV_ROWS + 2 * PAD

    def fill(i, c):
        s = pl.multiple_of(i * CONV_ROWS, CONV_ROWS)
        pad_ref[pl.ds(s + PAD, CONV_ROWS), :] = src_ref[0, pl.ds(s, CONV_ROWS), :].astype(F32)
        return c

    lax.fori_loop(0, n_rows // CONV_ROWS, fill, 0)

    def step(i, c):
        s = pl.multiple_of(i * CONV_ROWS, CONV_ROWS)
        w = pad_ref[pl.ds(s, win), :]
        prev = pltpu.roll(w, 1, 0)[PAD:PAD + CONV_ROWS]
        nxt = pltpu.roll(w, win - 1, 0)[PAD:PAD + CONV_ROWS]
        cur = w[PAD:PAD + CONV_ROWS]
        y = prev * w_ref[0:1, :] + cur * w_ref[1:2, :] + nxt * w_ref[2:3, :] + b_ref[...]
        dst_ref[pl.ds(s, CONV_ROWS), :] = _silu(y)
        return c

    lax.fori_loop(0, n_rows // CONV_ROWS, step, 0)


def _tri_dot(tri, v):
    hi, mid, lo = _split3(v)
    return _dot(tri, hi) + _dot(tri, mid) + _dot(tri, lo)


def _expand(v, e_mat):
    hi, lo = _split2(v)
    return _dot(hi, e_mat) + _dot(lo, e_mat)


def _ssd_dir_step(xs, bmat, cmat, cums, dts, st_ref, e_mat, is_fwd):
    edge = cums[CHUNK - 1:CHUNK, :] if is_fwd else cums[0:1, :]
    wdec = jnp.exp(edge - cums) * dts
    xw = (xs * _expand(wdec, e_mat)).astype(BF16)
    s_loc = _dot(bmat.T.astype(BF16), xw)
    tot = _expand(jnp.broadcast_to(jnp.exp(edge), (8, LANES)), e_mat)[0:1, :]
    s_prev = st_ref[...]
    y_off = None
    if cmat is not None:
        y_off = _dot(cmat.astype(BF16), s_prev.astype(BF16)) * _expand(jnp.exp(cums), e_mat)
    st_ref[...] = s_prev * tot + s_loc
    return y_off


def _ssd_diag(xs, cb, cum, rc, dt):
    cum_t, rc_t, dt_t = cum.T, rc.T, dt.T
    ii = lax.broadcasted_iota(jnp.int32, (CHUNK, CHUNK), 0)
    jj = lax.broadcasted_iota(jnp.int32, (CHUNK, CHUNK), 1)
    lower, upper = ii >= jj, jj >= ii
    first_half = jj < SSD_HEAD_DIM
    outs = []
    for pair in range(HEADS_PER_GROUP // 2):
        ms = []
        for h in (2 * pair, 2 * pair + 1):
            hb = HEADS_PER_GROUP + h
            lf = jnp.exp(jnp.where(lower, cum[:, h:h + 1] - cum_t[h:h + 1, :], NEG_BIG)) * dt_t[h:h + 1, :]
            lb = jnp.exp(jnp.where(upper, rc[:, hb:hb + 1] - rc_t[hb:hb + 1, :], NEG_BIG)) * dt_t[hb:hb + 1, :]
            ms.append((cb * (lf + lb)).astype(BF16))
        xp = xs[:, pair * LANES:(pair + 1) * LANES]
        rhs = jnp.concatenate([jnp.where(first_half, xp, 0.0), jnp.where(first_half, 0.0, xp)], axis=0)
        outs.append(_dot(jnp.concatenate(ms, axis=1), rhs.astype(BF16)))
    return jnp.concatenate(outs, axis=1)


def _ssd_body(xl_ref, bl_ref, cl_ref, zl_ref, dtl_ref, xc_ref, bc_ref, dtc_ref,
              cwx_ref, cwb_ref, cwc_ref, cbx_ref, cbb_ref, cbc_ref,
              dtbias_ref, alog_ref, dskip_ref, ng_ref,
              o_ref,
              xpad, bpad, xs_scr, bs_scr, cs_scr, a_scr, dt_scr, rc_scr,
              xcpad, bcpad, xsc_scr, bsc_scr, ac_scr, dtcs_scr, y_scr, st_scr):
    t_lat = xs_scr.shape[0]
    t_ctx = xsc_scr.shape[0]
    n_lat = t_lat // CHUNK
    n_ctx = t_ctx // CHUNK

    _conv3_silu(xl_ref, xpad, cwx_ref, cbx_ref, xs_scr, t_lat)
    _conv3_silu(bl_ref, bpad, cwb_ref, cbb_ref, bs_scr, t_lat)
    _conv3_silu(cl_ref, bpad, cwc_ref, cbc_ref, cs_scr, t_lat)
    _conv3_silu(xc_ref, xcpad, cwx_ref, cbx_ref, xsc_scr, t_ctx)
    _conv3_silu(bc_ref, bcpad, cwb_ref, cbb_ref, bsc_scr, t_ctx)
    neg_a = -jnp.exp(alog_ref[...])
    dt = jax.nn.softplus(dtl_ref[0] + dtbias_ref[...])
    dt_scr[...] = dt
    a_scr[...] = dt * neg_a
    dtc = jax.nn.softplus(dtc_ref[0] + dtbias_ref[...])
    dtcs_scr[...] = dtc
    ac_scr[...] = dtc * neg_a

    ii = lax.broadcasted_iota(jnp.int32, (CHUNK, CHUNK), 0)
    jj = lax.broadcasted_iota(jnp.int32, (CHUNK, CHUNK), 1)
    tril = (ii >= jj).astype(BF16)
    triu = (jj >= ii).astype(BF16)
    er = lax.broadcasted_iota(jnp.int32, (LANES, D_GROUP), 0)
    eh = lax.broadcasted_iota(jnp.int32, (LANES, D_GROUP), 1) // SSD_HEAD_DIM
    e_f = (er == eh).astype(BF16)
    e_b = (er == eh + HEADS_PER_GROUP).astype(BF16)

    def ctx_step(c, is_fwd):
        s = pl.multiple_of(c * CHUNK, CHUNK)
        cums = _tri_dot(tril if is_fwd else triu, ac_scr[pl.ds(s, CHUNK), :])
        _ssd_dir_step(xsc_scr[pl.ds(s, CHUNK), :], bsc_scr[pl.ds(s, CHUNK), :], None, cums,
                      dtcs_scr[pl.ds(s, CHUNK), :], st_scr, e_f if is_fwd else e_b, is_fwd)

    def lat_fwd(c, carry):
        s = pl.multiple_of(c * CHUNK, CHUNK)
        a_blk = a_scr[pl.ds(s, CHUNK), :]
        dt_blk = dt_scr[pl.ds(s, CHUNK), :]
        cum = _tri_dot(tril, a_blk)
        rc = _tri_dot(triu, a_blk)
        rc_scr[pl.ds(s, CHUNK), :] = rc
        xs = xs_scr[pl.ds(s, CHUNK), :]
        bm = bs_scr[pl.ds(s, CHUNK), :]
        cm = cs_scr[pl.ds(s, CHUNK), :]
        cb = lax.dot_general(cm.astype(BF16), bm.astype(BF16), (((1,), (1,)), ((), ())),
                             preferred_element_type=F32)
        y_diag = _ssd_diag(xs, cb, cum, rc, dt_blk)
        y_off = _ssd_dir_step(xs, bm, cm, cum, dt_blk, st_scr, e_f, True)
        y_scr[pl.ds(s, CHUNK), :] = y_diag + y_off
        return carry

    def lat_bwd(k, carry):
        s = pl.multiple_of((n_lat - 1 - k) * CHUNK, CHUNK)
        y_off = _ssd_dir_step(xs_scr[pl.ds(s, CHUNK), :], bs_scr[pl.ds(s, CHUNK), :], cs_scr[pl.ds(s, CHUNK), :],
                              rc_scr[pl.ds(s, CHUNK), :], dt_scr[pl.ds(s, CHUNK), :], st_scr, e_b, False)
        y_scr[pl.ds(s, CHUNK), :] += y_off
        return carry

    st_scr[...] = jnp.zeros(st_scr.shape, F32)
    for c in range(n_ctx):
        ctx_step(c, True)
    lax.fori_loop(0, n_lat, lat_fwd, 0)
    st_scr[...] = jnp.zeros(st_scr.shape, F32)
    for c in reversed(range(n_ctx)):
        ctx_step(c, False)
    lax.fori_loop(0, n_lat, lat_bwd, 0)

    def finish(i, carry):
        s = pl.multiple_of(i * CONV_ROWS, CONV_ROWS)
        y = y_scr[pl.ds(s, CONV_ROWS), :] + dskip_ref[...] * xs_scr[pl.ds(s, CONV_ROWS), :]
        y = y * _silu(zl_ref[0, pl.ds(s, CONV_ROWS), :].astype(F32))
        ms = jnp.mean(y * y, axis=-1, keepdims=True)
        o_ref[0, pl.ds(s, CONV_ROWS), :] = (y * lax.rsqrt(ms + EPS) * ng_ref[...]).astype(o_ref.dtype)
        return carry

    lax.fori_loop(0, t_lat // CONV_ROWS, finish, 0)


def _ssd(proj3, dt3, projc3, dtc3, conv_w, conv_b, dtbias_pk, alog_pk, dskip_row, norm_g_row):
    bsz, t_lat, _ = proj3.shape
    t_ctx = projc3.shape[1]
    xb, bb, cb_ = PK_X // D_GROUP, PK_B // D_STATE, PK_C // D_STATE
    cxb, cbb = 0, D_SSD // D_STATE
    wx0, wb0, wc0 = 0, D_SSD // D_STATE, (D_SSD + D_BC) // D_STATE
    in_specs = [
        pl.BlockSpec((1, t_lat, D_GROUP), lambda b, g: (b, 0, xb + g)),
        pl.BlockSpec((1, t_lat, D_STATE), lambda b, g: (b, 0, bb + g)),
        pl.BlockSpec((1, t_lat, D_STATE), lambda b, g: (b, 0, cb_ + g)),
        pl.BlockSpec((1, t_lat, D_GROUP), lambda b, g: (b, 0, g)),
        pl.BlockSpec((1, t_lat, LANES), lambda b, g: (b, 0, g)),
        pl.BlockSpec((1, t_ctx, D_GROUP), lambda b, g: (b, 0, cxb + g)),
        pl.BlockSpec((1, t_ctx, D_STATE), lambda b, g: (b, 0, cbb + g)),
        pl.BlockSpec((1, t_ctx, LANES), lambda b, g: (b, 0, g)),
        pl.BlockSpec((3, D_GROUP), lambda b, g: (0, wx0 + g)),
        pl.BlockSpec((3, D_STATE), lambda b, g: (0, wb0 + g)),
        pl.BlockSpec((3, D_STATE), lambda b, g: (0, wc0 + g)),
        pl.BlockSpec((1, D_GROUP), lambda b, g: (0, wx0 + g)),
        pl.BlockSpec((1, D_STATE), lambda b, g: (0, wb0 + g)),
        pl.BlockSpec((1, D_STATE), lambda b, g: (0, wc0 + g)),
        pl.BlockSpec((1, LANES), lambda b, g: (0, g)),
        pl.BlockSpec((1, LANES), lambda b, g: (0, g)),
        pl.BlockSpec((1, D_GROUP), lambda b, g: (0, g)),
        pl.BlockSpec((1, D_GROUP), lambda b, g: (0, g)),
    ]
    scratch = [
        pltpu.VMEM((t_lat + 2 * PAD, D_GROUP), F32),
        pltpu.VMEM((t_lat + 2 * PAD, D_STATE), F32),
        pltpu.VMEM((t_lat, D_GROUP), F32),
        pltpu.VMEM((t_lat, D_STATE), F32),
        pltpu.VMEM((t_lat, D_STATE), F32),
        pltpu.VMEM((t_lat, LANES), F32),
        pltpu.VMEM((t_lat, LANES), F32),
        pltpu.VMEM((t_lat, LANES), F32),
        pltpu.VMEM((t_ctx + 2 * PAD, D_GROUP), F32),
        pltpu.VMEM((t_ctx + 2 * PAD, D_STATE), F32),
        pltpu.VMEM((t_ctx, D_GROUP), F32),
        pltpu.VMEM((t_ctx, D_STATE), F32),
        pltpu.VMEM((t_ctx, LANES), F32),
        pltpu.VMEM((t_ctx, LANES), F32),
        pltpu.VMEM((t_lat, D_GROUP), F32),
        pltpu.VMEM((D_STATE, D_GROUP), F32),
    ]
    return pl.pallas_call(
        _ssd_body,
        grid=(bsz, SSD_GROUPS),
        in_specs=in_specs,
        out_specs=pl.BlockSpec((1, t_lat, D_GROUP), lambda b, g: (b, 0, g)),
        out_shape=jax.ShapeDtypeStruct((bsz, t_lat, D_SSD), BF16),
        scratch_shapes=scratch,
        compiler_params=_cparams(("arbitrary", "arbitrary")),
        name="ssd_bidir",
    )(proj3, proj3, proj3, proj3, dt3, projc3, projc3, dtc3,
      conv_w, conv_w, conv_w, conv_b, conv_b, conv_b, dtbias_pk, alog_pk, dskip_row, norm_g_row)


def _conf_body(a_ref, g_ref, w_ref, b_ref, o_ref, v_scr, p_scr, c_scr, *, along_w):
    t = v_scr.shape[0]
    rows = t // GRID_W
    minor = rows if along_w else GRID_W
    halo = CONF_HALF * minor
    tc = v_scr.shape[1]

    def glu(i, c):
        s = pl.multiple_of(i * CONV_ROWS, CONV_ROWS)
        a = a_ref[0, pl.ds(s, CONV_ROWS), :].astype(F32)
        g = g_ref[0, pl.ds(s, CONV_ROWS), :].astype(F32)
        dst = v_scr if along_w else p_scr
        off = 0 if along_w else halo
        dst[pl.ds(s + off, CONV_ROWS), :] = a * jax.nn.sigmoid(g)
        return c

    lax.fori_loop(0, t // CONV_ROWS, glu, 0)
    p_scr[0:halo, :] = jnp.zeros((halo, tc), F32)
    p_scr[halo + t:halo + t + halo, :] = jnp.zeros((halo, tc), F32)
    if along_w:
        for w in range(GRID_W):
            p_scr[halo + w * rows:halo + (w + 1) * rows, :] = v_scr[pl.ds(w, rows, stride=GRID_W), :]

    def conv(i, c):
        s = pl.multiple_of(i * CONF_ROWS, CONF_ROWS)
        acc = jnp.broadcast_to(b_ref[...], (CONF_ROWS, tc))
        for k in range(CONF_KERNEL):
            acc = acc + w_ref[k:k + 1, :] * p_scr[pl.ds(s + k * minor, CONF_ROWS), :]
        if along_w:
            c_scr[pl.ds(s, CONF_ROWS), :] = acc
        else:
            o_ref[0, pl.ds(s, CONF_ROWS), :] = acc.astype(o_ref.dtype)
        return c

    lax.fori_loop(0, t // CONF_ROWS, conv, 0)
    if along_w:
        for r in range(rows):
            o_ref[0, r * GRID_W:(r + 1) * GRID_W, :] = c_scr[pl.ds(r, GRID_W, stride=rows), :].astype(o_ref.dtype)


def _conf_conv(proj3, dw_w, dw_b_row, along_w):
    bsz, t, _ = proj3.shape
    rows = t // GRID_W
    half = D_CONF // 2
    c0 = 0 if along_w else half
    a0, g0, w0 = (PK_CA + c0) // CONF_TC, (PK_CG + c0) // CONF_TC, c0 // CONF_TC
    halo = CONF_HALF * (rows if along_w else GRID_W)
    return pl.pallas_call(
        functools.partial(_conf_body, along_w=along_w),
        grid=(bsz, half // CONF_TC),
        in_specs=[
            pl.BlockSpec((1, t, CONF_TC), lambda b, j: (b, 0, a0 + j)),
            pl.BlockSpec((1, t, CONF_TC), lambda b, j: (b, 0, g0 + j)),
            pl.BlockSpec((CONF_KERNEL, CONF_TC), lambda b, j: (0, w0 + j)),
            pl.BlockSpec((1, CONF_TC), lambda b, j: (0, w0 + j)),
        ],
        out_specs=pl.BlockSpec((1, t, CONF_TC), lambda b, j: (b, 0, j)),
        out_shape=jax.ShapeDtypeStruct((bsz, t, half), BF16),
        scratch_shapes=[
            pltpu.VMEM((t, CONF_TC), F32),
            pltpu.VMEM((t + 2 * halo, CONF_TC), F32),
            pltpu.VMEM((t, CONF_TC), F32),
        ],
        compiler_params=_cparams(("arbitrary", "arbitrary")),
        name="conf_conv_w" if along_w else "conf_conv_r",
    )(proj3, proj3, dw_w, dw_b_row)


def _outproj_body(ssd_ref, cw_ref, cr_ref, x_ref, wt_ref, wb_ref, lng_ref, lnb_ref, gm_ref,
                  ng_ref, shf_ref, scf_ref, wrh_ref, wrl_ref, br_ref,
                  x1_ref, h2_ref, route_ref, cnt_ref, cnt_scr):
    i = pl.program_id(0)
    tm = x_ref.shape[0]

    @pl.when(i == 0)
    def _():
        cnt_scr[...] = jnp.zeros(cnt_scr.shape, F32)

    v = jnp.concatenate([cw_ref[...].astype(F32), cr_ref[...].astype(F32)], axis=1)
    mu = jnp.mean(v, axis=-1, keepdims=True)
    vc = v - mu
    var = jnp.mean(vc * vc, axis=-1, keepdims=True)
    cf = _silu(vc * lax.rsqrt(var + EPS) * lng_ref[...] + lnb_ref[...])
    acc = _dot(ssd_ref[...], wt_ref[...]) + _dot(cf.astype(BF16), wb_ref[...])
    x1 = x_ref[...] + gm_ref[0] * acc
    x1_ref[...] = x1

    ms = jnp.mean(x1 * x1, axis=-1, keepdims=True)
    h2 = x1 * lax.rsqrt(ms + EPS) * ng_ref[...] * (1.0 + scf_ref[0]) + shf_ref[0]
    h2_ref[...] = h2

    hh, hl = _split2(h2)
    logit = _dot(hh, wrh_ref[...]) + _dot(hl, wrh_ref[...]) + _dot(hh, wrl_ref[...]) + br_ref[...]
    lane = lax.broadcasted_iota(jnp.int32, (tm, LANES), 1).astype(F32)
    tops, hots, idxs = [], [], []
    for _ in range(TOP_K):
        m = jnp.max(logit, axis=-1, keepdims=True)
        idx = jnp.min(jnp.where(logit == m, lane, float(LANES)), axis=-1, keepdims=True)
        hot = lane == idx
        tops.append(m)
        hots.append(hot)
        idxs.append(idx)
        logit = jnp.where(hot, -jnp.inf, logit)
    es = [jnp.exp(m - tops[0]) for m in tops]
    den = es[0] + es[1] + es[2] + es[3]

    sel = jnp.zeros((tm, LANES), F32)
    for hot in hots:
        sel = sel + hot.astype(F32)
    ii = lax.broadcasted_iota(jnp.int32, (tm, tm), 0)
    jj = lax.broadcasted_iota(jnp.int32, (tm, tm), 1)
    before = _dot((ii > jj).astype(BF16), sel.astype(BF16)) + cnt_scr[0:1, :]
    packed = jnp.zeros((tm, LANES), F32)
    for k in range(TOP_K):
        pos = jnp.sum(jnp.where(hots[k], before, 0.0), axis=-1, keepdims=True)
        packed = packed + jnp.where(lane == float(k), idxs[k], 0.0)
        packed = packed + jnp.where(lane == float(TOP_K + k), es[k] / den, 0.0)
        packed = packed + jnp.where(lane == float(2 * TOP_K + k), pos, 0.0)
    route_ref[...] = packed
    cnt_scr[...] = cnt_scr[...] + jnp.sum(sel, axis=0, keepdims=True)
    cnt_ref[...] = cnt_scr[...]


def _out_proj(ssd2, confw2, confr2, x2, w_out_bf, ln_g, ln_b, g_m, ffn_g, sh_f, sc_f, wr_hi, wr_lo, br_pad,
              rows_per_mod):
    m = x2.shape[0]
    tm = OUT_TM
    per = rows_per_mod // tm
    half = D_CONF // 2
    row = lambda i: (i, 0)
    fixed = lambda i: (0, 0)
    mod = lambda i: (i // per, 0, 0)
    return pl.pallas_call(
        _outproj_body,
        grid=(m // tm,),
        in_specs=[
            pl.BlockSpec((tm, D_SSD), row),
            pl.BlockSpec((tm, half), row),
            pl.BlockSpec((tm, half), row),
            pl.BlockSpec((tm, D_MODEL), row),
            pl.BlockSpec((D_SSD, D_MODEL), lambda i: (0, 0), pipeline_mode=pl.Buffered(1)),
            pl.BlockSpec((D_CONF, D_MODEL), lambda i: (1, 0), pipeline_mode=pl.Buffered(1)),
            pl.BlockSpec((1, D_CONF), fixed),
            pl.BlockSpec((1, D_CONF), fixed),
            pl.BlockSpec((1, 1, D_MODEL), mod),
            pl.BlockSpec((1, D_MODEL), fixed),
            pl.BlockSpec((1, 1, D_MODEL), mod),
            pl.BlockSpec((1, 1, D_MODEL), mod),
            pl.BlockSpec((D_MODEL, LANES), fixed),
            pl.BlockSpec((D_MODEL, LANES), fixed),
            pl.BlockSpec((1, LANES), fixed),
        ],
        out_specs=[
            pl.BlockSpec((tm, D_MODEL), row),
            pl.BlockSpec((tm, D_MODEL), row),
            pl.BlockSpec((tm, LANES), row),
            pl.BlockSpec((8, LANES), fixed),
        ],
        out_shape=[
            jax.ShapeDtypeStruct((m, D_MODEL), F32),
            jax.ShapeDtypeStruct((m, D_MODEL), F32),
            jax.ShapeDtypeStruct((m, LANES), F32),
            jax.ShapeDtypeStruct((8, LANES), F32),
        ],
        scratch_shapes=[pltpu.VMEM((8, LANES), F32)],
        compiler_params=_cparams(("arbitrary",)),
        name="out_proj_route",
    )(ssd2, confw2, confr2, x2, w_out_bf, w_out_bf, ln_g, ln_b, g_m, ffn_g, sh_f, sc_f, wr_hi, wr_lo, br_pad)


def _gather_body(nsub_ref, bi_ref, tok_ref, src_ref, o_ref, sem):
    i = pl.program_id(0)

    @pl.when(nsub_ref[i] > 0)
    def _():
        def row_copy(tok, r):
            return pltpu.make_async_copy(src_ref.at[pl.ds(tok, 1)], o_ref.at[pl.ds(r, 1)], sem)

        def issue(r, c):
            row_copy(tok_ref[0, 0, r], r).start()
            return c

        lax.fori_loop(0, MOE_SUPER, issue, 0)

        def drain(r, c):
            row_copy(0, r).wait()
            return c

        lax.fori_loop(0, MOE_SUPER, drain, 0)


def _dispatch(h2p, slot_tok3, nsub, block_idx, n_super):
    width = h2p.shape[1]
    return pl.pallas_call(
        _gather_body,
        grid_spec=pltpu.PrefetchScalarGridSpec(
            num_scalar_prefetch=2,
            grid=(n_super,),
            in_specs=[
                pl.BlockSpec((1, 1, MOE_SUPER), lambda i, ns, bi: (bi[i], 0, 0), memory_space=pltpu.SMEM),
                pl.BlockSpec(memory_space=pl.ANY),
            ],
            out_specs=pl.BlockSpec((MOE_SUPER, width), lambda i, ns, bi: (bi[i], 0)),
            scratch_shapes=[pltpu.SemaphoreType.DMA(())],
        ),
        out_shape=jax.ShapeDtypeStruct((n_super * MOE_SUPER, width), h2p.dtype),
        compiler_params=_cparams(("arbitrary",)),
        name="moe_dispatch",
    )(nsub, block_idx, slot_tok3, h2p)


def _moe_body(be_ref, nsub_ref, bi_ref, x_ref, wg_ref, wl_ref, bg_ref, bl_ref, wd_ref, bd_ref, o_ref,
              wg_scr, wl_scr, wd_scr):
    i = pl.program_id(0)
    f = pl.program_id(1)
    ns = nsub_ref[i]

    @pl.when(ns > 0)
    def _():
        @pl.when(f == 0)
        def _():
            o_ref[...] = jnp.broadcast_to(bd_ref[0], o_ref.shape)

        wg_scr[...] = wg_ref[0].astype(BF16)
        wl_scr[...] = wl_ref[0].astype(BF16)
        wd_scr[...] = wd_ref[0].astype(BF16)

        def sub(s, c):
            r0 = pl.multiple_of(s * MOE_SUB, MOE_SUB)
            xb = x_ref[pl.ds(r0, MOE_SUB), :].astype(BF16)
            gate = _dot(xb, wg_scr[...]) + bg_ref[0]
            lin = _dot(xb, wl_scr[...]) + bl_ref[0]
            glu = jnp.minimum(gate, SWIGLU_LIMIT)
            lin = jnp.clip(lin, -SWIGLU_LIMIT, SWIGLU_LIMIT)
            act = glu * jax.nn.sigmoid(SWIGLU_ALPHA * glu) * (lin + 1.0)
            o_ref[pl.ds(r0, MOE_SUB), :] += _dot(act.astype(BF16), wd_scr[...])
            return c

        lax.fori_loop(0, ns, sub, 0)


def _moe_mlp(xs, block_e, nsub, block_idx, w_gate_up, b_gate_up3, w_down, b_down3, n_super):
    n_ff = D_FF // MOE_TF
    lin0 = D_FF // MOE_TF

    def ff(i, f, ns):
        return jnp.where(ns[i] > 0, f, n_ff - 1)

    return pl.pallas_call(
        _moe_body,
        grid_spec=pltpu.PrefetchScalarGridSpec(
            num_scalar_prefetch=3,
            grid=(n_super, n_ff),
            in_specs=[
                pl.BlockSpec((MOE_SUPER, D_MODEL), lambda i, f, be, ns, bi: (bi[i], 0)),
                pl.BlockSpec((1, D_MODEL, MOE_TF), lambda i, f, be, ns, bi: (be[i], 0, ff(i, f, ns))),
                pl.BlockSpec((1, D_MODEL, MOE_TF), lambda i, f, be, ns, bi: (be[i], 0, lin0 + ff(i, f, ns))),
                pl.BlockSpec((1, 1, MOE_TF), lambda i, f, be, ns, bi: (be[i], 0, ff(i, f, ns))),
                pl.BlockSpec((1, 1, MOE_TF), lambda i, f, be, ns, bi: (be[i], 0, lin0 + ff(i, f, ns))),
                pl.BlockSpec((1, MOE_TF, D_MODEL), lambda i, f, be, ns, bi: (be[i], ff(i, f, ns), 0)),
                pl.BlockSpec((1, 1, D_MODEL), lambda i, f, be, ns, bi: (be[i], 0, 0)),
            ],
            out_specs=pl.BlockSpec((MOE_SUPER, D_MODEL), lambda i, f, be, ns, bi: (bi[i], 0)),
            scratch_shapes=[
                pltpu.VMEM((D_MODEL, MOE_TF), BF16),
                pltpu.VMEM((D_MODEL, MOE_TF), BF16),
                pltpu.VMEM((MOE_TF, D_MODEL), BF16),
            ],
        ),
        out_shape=jax.ShapeDtypeStruct((n_super * MOE_SUPER, D_MODEL), F32),
        compiler_params=_cparams(("arbitrary", "arbitrary")),
        name="moe_mlp",
    )(block_e, nsub, block_idx, xs, w_gate_up, w_gate_up, b_gate_up3, b_gate_up3, w_down, b_down3)


def _combine_body(dest_ref, y_ref, x1_ref, route_ref, gf_ref, fg_ref, o_ref, buf, sem):
    tm = x1_ref.shape[0]

    def issue(r, c):
        for k in range(TOP_K):
            pltpu.make_async_copy(y_ref.at[pl.ds(dest_ref[0, 0, r * TOP_K + k], 1)], buf.at[k, pl.ds(r, 1)], sem).start()
        return c

    lax.fori_loop(0, tm, issue, 0)

    def drain(r, c):
        for k in range(TOP_K):
            pltpu.make_async_copy(y_ref.at[pl.ds(0, 1)], buf.at[k, pl.ds(0, 1)], sem).wait()
        return c

    lax.fori_loop(0, tm, drain, 0)

    route = route_ref[...]
    moe = jnp.zeros(x1_ref.shape, F32)
    for k in range(TOP_K):
        moe = moe + route[:, TOP_K + k:TOP_K + k + 1] * buf[k]
    x2 = x1_ref[...] + gf_ref[0] * moe
    ms = jnp.mean(x2 * x2, axis=-1, keepdims=True)
    o_ref[...] = x2 * lax.rsqrt(ms + EPS) * fg_ref[...]


def _combine(dest3, y, x1, route, g_f, final_g, rows_per_mod):
    m = x1.shape[0]
    tm = COMB_TM
    per = rows_per_mod // tm
    return pl.pallas_call(
        _combine_body,
        grid=(m // tm,),
        in_specs=[
            pl.BlockSpec((1, 1, tm * TOP_K), lambda i: (i, 0, 0), memory_space=pltpu.SMEM),
            pl.BlockSpec(memory_space=pl.ANY),
            pl.BlockSpec((tm, D_MODEL), lambda i: (i, 0)),
            pl.BlockSpec((tm, LANES), lambda i: (i, 0)),
            pl.BlockSpec((1, 1, D_MODEL), lambda i: (i // per, 0, 0)),
            pl.BlockSpec((1, D_MODEL), lambda i: (0, 0)),
        ],
        out_specs=pl.BlockSpec((tm, D_MODEL), lambda i: (i, 0)),
        out_shape=jax.ShapeDtypeStruct((m, D_MODEL), F32),
        scratch_shapes=[pltpu.VMEM((TOP_K, tm, D_MODEL), F32), pltpu.SemaphoreType.DMA(())],
        compiler_params=_cparams(("arbitrary",)),
        name="moe_combine_norm",
    )(dest3, y, x1, route, g_f, final_g)


def _pack_heads(fwd, bwd):
    f = fwd.reshape(SSD_GROUPS, HEADS_PER_GROUP)
    b = bwd.reshape(SSD_GROUPS, HEADS_PER_GROUP)
    z = jnp.zeros((SSD_GROUPS, LANES - 2 * HEADS_PER_GROUP), F32)
    return jnp.concatenate([f, b, z], axis=1).reshape(1, SSD_GROUPS * LANES)


def _routing_tables(route, cnt, n_tok, n_super):
    idx = route[:, 0:TOP_K].astype(jnp.int32)
    pos = route[:, 2 * TOP_K:3 * TOP_K].astype(jnp.int32)
    counts = cnt[0, :N_EXPERTS].astype(jnp.int32)
    padded = (counts + MOE_SUPER - 1) // MOE_SUPER * MOE_SUPER
    pad_end = jnp.cumsum(padded)
    pad_start = pad_end - padded
    dest = pad_start[idx] + pos
    n_live = pad_end[-1] // MOE_SUPER
    block_idx = jnp.minimum(jnp.arange(n_super, dtype=jnp.int32), n_live - 1)
    blk_start = block_idx * MOE_SUPER
    block_e = jnp.sum((pad_end[None, :] <= blk_start[:, None]).astype(jnp.int32), axis=1)
    block_e = jnp.minimum(block_e, N_EXPERTS - 1)
    live_rows = jnp.clip(counts[block_e] - (blk_start - pad_start[block_e]), 0, MOE_SUPER)
    nsub = (live_rows + MOE_SUB - 1) // MOE_SUB
    nsub = jnp.where(jnp.arange(n_super) < n_live, nsub, 0).astype(jnp.int32)
    tok = jnp.broadcast_to(jnp.arange(n_tok, dtype=jnp.int32)[:, None], (n_tok, TOP_K))
    slot_tok = jnp.zeros((n_super * MOE_SUPER,), jnp.int32).at[dest.reshape(-1)].set(tok.reshape(-1))
    return dest, block_e, nsub, block_idx, slot_tok


def kernel(x, c, ctx, c_ctx, w_mod, b_mod, norm_mix_g, norm_ffn_g, w_in, ssd_conv_w, ssd_conv_b, ssd_a_log_f, ssd_dt_bias_f, ssd_a_log_b, ssd_dt_bias_b, ssd_d_skip, ssd_norm_g, conf_dw_w, conf_dw_b, conf_ln_g, conf_ln_b, w_out, w_router, b_router, w_gate_up, b_gate_up, w_down, b_down, final_norm_g):
    bsz, n_lat, _ = x.shape
    n_ctx = ctx.shape[1]
    n_tok = bsz * n_lat
    l = 0

    c8 = jnp.zeros((8, D_MODEL), F32).at[:bsz].set(c).at[bsz].set(c_ctx)
    mod = _adaln_mod(c8, w_mod[l], b_mod[l][None, :])
    sh_m, sc_m, g_m, sh_f, sc_f, g_f = [mod[:bsz, k * D_MODEL:(k + 1) * D_MODEL].reshape(bsz, 1, D_MODEL)
                                        for k in range(N_MOD)]
    shc = mod[bsz, 0:D_MODEL].reshape(1, 1, D_MODEL)
    scc = mod[bsz, D_MODEL:2 * D_MODEL].reshape(1, 1, D_MODEL)

    wi = w_in[l]
    w_main = jnp.concatenate([wi[:, :OFF_DT], wi[:, OFF_CONF:]], axis=1).astype(BF16)
    wf = wi[:, OFF_DT:OFF_DT + SSD_HEADS].reshape(D_MODEL, SSD_GROUPS, HEADS_PER_GROUP)
    wb = wi[:, OFF_DT + SSD_HEADS:OFF_CONF].reshape(D_MODEL, SSD_GROUPS, HEADS_PER_GROUP)
    wz = jnp.zeros((D_MODEL, SSD_GROUPS, LANES - 2 * HEADS_PER_GROUP), F32)
    w_dt = jnp.concatenate([wf, wb, wz], axis=2).reshape(D_MODEL, SSD_GROUPS * LANES).astype(BF16)

    gain_mix = norm_mix_g[l][None, :]
    proj, dt_raw = _in_proj(x.reshape(n_tok, D_MODEL), gain_mix, sh_m, sc_m, w_main, w_dt,
                            0, PK_N // PROJ_TN, 1024, n_lat)
    n_cb = (D_SSD + 2 * D_BC) // PROJ_TN
    projc, dtc_raw = _in_proj(ctx.reshape(bsz * n_ctx, D_MODEL), gain_mix, shc, scc, w_main, w_dt,
                              PK_X // PROJ_TN, n_cb, bsz * n_ctx, bsz * n_ctx)
    proj3 = proj.reshape(bsz, n_lat, PK_N)

    ssd_out = _ssd(proj3, dt_raw.reshape(bsz, n_lat, -1), projc.reshape(bsz, n_ctx, -1),
                   dtc_raw.reshape(bsz, n_ctx, -1), ssd_conv_w[l], ssd_conv_b[l][None, :],
                   _pack_heads(ssd_dt_bias_f[l], ssd_dt_bias_b[l]), _pack_heads(ssd_a_log_f[l], ssd_a_log_b[l]),
                   jnp.repeat(ssd_d_skip[l], SSD_HEAD_DIM)[None, :], ssd_norm_g[l][None, :])

    conf_w = _conf_conv(proj3, conf_dw_w[l], conf_dw_b[l][None, :], True)
    conf_r = _conf_conv(proj3, conf_dw_w[l], conf_dw_b[l][None, :], False)

    wr = jnp.zeros((D_MODEL, LANES), F32).at[:, :N_EXPERTS].set(w_router[l])
    wr_hi = wr.astype(BF16)
    wr_lo = (wr - wr_hi.astype(F32)).astype(BF16)
    br = jnp.full((1, LANES), NEG_BIG, F32).at[0, :N_EXPERTS].set(b_router[l])
    half = D_CONF // 2
    x1, h2p, route, cnt = _out_proj(
        ssd_out.reshape(n_tok, D_SSD), conf_w.reshape(n_tok, half), conf_r.reshape(n_tok, half),
        x.reshape(n_tok, D_MODEL), w_out[l].astype(BF16), conf_ln_g[l][None, :], conf_ln_b[l][None, :], g_m,
        norm_ffn_g[l][None, :], sh_f, sc_f, wr_hi, wr_lo, br, n_lat)

    n_super = n_tok * TOP_K // MOE_SUPER + N_EXPERTS
    dest, block_e, nsub, block_idx, slot_tok = _routing_tables(route, cnt, n_tok, n_super)
    xs = _dispatch(h2p, slot_tok.reshape(n_super, 1, MOE_SUPER), nsub, block_idx, n_super)
    y = _moe_mlp(xs, block_e, nsub, block_idx, w_gate_up[l], b_gate_up[l][:, None, :], w_down[l], b_down[l][:, None, :], n_super)
    out = _combine(dest.reshape(n_tok // COMB_TM, 1, COMB_TM * TOP_K), y, x1, route, g_f,
                   final_norm_g[None, :], n_lat)
    return out.reshape(bsz, n_lat, D_MODEL)
```

```python
import functools

import jax
import jax.numpy as jnp
from jax import lax
from jax.experimental import pallas as pl
from jax.experimental.pallas import tpu as pltpu

F32 = jnp.float32
BF16 = jnp.bfloat16

D_MODEL = 2048
N_MOD = 6
EPS = 1e-6
GRID_W = 64

SSD_HEADS = 32
SSD_HEAD_DIM = 64
D_SSD = SSD_HEADS * SSD_HEAD_DIM
SSD_GROUPS = 4
HEADS_PER_GROUP = SSD_HEADS // SSD_GROUPS
D_GROUP = D_SSD // SSD_GROUPS
D_STATE = 128
CHUNK = 128
D_BC = SSD_GROUPS * D_STATE
D_CONF = 2048
CONF_KERNEL = 31
CONF_HALF = CONF_KERNEL // 2
N_EXPERTS = 32
TOP_K = 4
D_FF = D_MODEL
SWIGLU_LIMIT = 7.0
SWIGLU_ALPHA = 1.702

OFF_XBC = D_SSD
OFF_DT = OFF_XBC + D_SSD + 2 * D_BC
OFF_CONF = OFF_DT + 2 * SSD_HEADS

PK_Z = 0
PK_X = D_SSD
PK_B = PK_X + D_SSD
PK_C = PK_B + D_BC
PK_CA = PK_C + D_BC
PK_CG = PK_CA + D_CONF
PK_N = PK_CG + D_CONF

LANES = 128
VMEM_LIMIT_BYTES = 56 * 1024 * 1024
NEG_BIG = -1e30

PROJ_TN = 512
MOE_SUPER = 1024
MOE_SUB = 256
MOE_NSUB = MOE_SUPER // MOE_SUB
MOE_TF = 256
OUT_TM = 256
COMB_TM = 256
CONF_TC = LANES
CONF_ROWS = 128


def _cparams(sem):
    return pltpu.CompilerParams(dimension_semantics=sem, vmem_limit_bytes=VMEM_LIMIT_BYTES)


def _silu(v):
    return v * jax.nn.sigmoid(v)


def _dot(a, b):
    return jnp.dot(a, b, preferred_element_type=F32)


def _split2(v):
    hi = v.astype(BF16)
    lo = (v - hi.astype(F32)).astype(BF16)
    return hi, lo


def _split3(v):
    hi = v.astype(BF16)
    r1 = v - hi.astype(F32)
    mid = r1.astype(BF16)
    lo = (r1 - mid.astype(F32)).astype(BF16)
    return hi, mid, lo


def _mod_body(c_ref, w_ref, b_ref, o_ref):
    s = _silu(c_ref[...])
    o_ref[...] = _dot(s.astype(BF16), w_ref[...].astype(BF16)) + b_ref[...]


def _adaln_mod(c8, w_mod, b_mod):
    tn = 1024
    n = w_mod.shape[1]
    return pl.pallas_call(
        _mod_body,
        grid=(n // tn,),
        in_specs=[
            pl.BlockSpec((8, D_MODEL), lambda j: (0, 0)),
            pl.BlockSpec((D_MODEL, tn), lambda j: (0, j)),
            pl.BlockSpec((1, tn), lambda j: (0, j)),
        ],
        out_specs=pl.BlockSpec((8, tn), lambda j: (0, j)),
        out_shape=jax.ShapeDtypeStruct((8, n), F32),
        compiler_params=_cparams(("arbitrary",)),
        name="adaln_mod",
    )(c8, w_mod, b_mod)


def _inproj_body(x_ref, g_ref, sh_ref, sc_ref, w_ref, wdt_ref, o_ref, odt_ref, h_scr):
    @pl.when(pl.program_id(1) == 0)
    def _():
        xf = x_ref[...]
        ms = jnp.mean(xf * xf, axis=-1, keepdims=True)
        y = xf * lax.rsqrt(ms + EPS) * g_ref[...]
        hb = (y * (1.0 + sc_ref[0]) + sh_ref[0]).astype(BF16)
        h_scr[...] = hb
        odt_ref[...] = _dot(hb, wdt_ref[...])

    o_ref[...] = _dot(h_scr[...], w_ref[...]).astype(o_ref.dtype)


def _in_proj(x2d, gain, shift, scale, w_main, w_dt, col_block0, n_col_blocks, tm, rows_per_mod):
    m = x2d.shape[0]
    mods_per_tile = rows_per_mod // tm
    n_dt = w_dt.shape[1]
    return pl.pallas_call(
        _inproj_body,
        grid=(m // tm, n_col_blocks),
        in_specs=[
            pl.BlockSpec((tm, D_MODEL), lambda i, j: (i, 0)),
            pl.BlockSpec((1, D_MODEL), lambda i, j: (0, 0)),
            pl.BlockSpec((1, 1, D_MODEL), lambda i, j: (i // mods_per_tile, 0, 0)),
            pl.BlockSpec((1, 1, D_MODEL), lambda i, j: (i // mods_per_tile, 0, 0)),
            pl.BlockSpec((D_MODEL, PROJ_TN), lambda i, j: (0, col_block0 + j)),
            pl.BlockSpec((D_MODEL, n_dt), lambda i, j: (0, 0)),
        ],
        out_specs=[
            pl.BlockSpec((tm, PROJ_TN), lambda i, j: (i, j)),
            pl.BlockSpec((tm, n_dt), lambda i, j: (i, 0)),
        ],
        out_shape=[
            jax.ShapeDtypeStruct((m, n_col_blocks * PROJ_TN), BF16),
            jax.ShapeDtypeStruct((m, n_dt), F32),
        ],
        scratch_shapes=[pltpu.VMEM((tm, D_MODEL), BF16)],
        compiler_params=_cparams(("arbitrary", "arbitrary")),
        name="in_proj",
    )(x2d, gain, shift, scale, w_main, w_dt)


CONV_ROWS = 256
PAD = 8


def _conv3_silu(src_ref, pad_ref, w_ref, b_ref, dst_ref, n_rows):
    ncol = pad_ref.shape[1]
    zero = jnp.zeros((PAD, ncol), F32)
    pad_ref[0:PAD, :] = zero
    pad_ref[PAD + n_rows:PAD + n_rows + PAD, :] = zero
    win = CONV_ROWS + 2 * PAD

    def fill(i, c):
        s = pl.multiple_of(i * CONV_ROWS, CONV_ROWS)
        pad_ref[pl.ds(s + PAD, CONV_ROWS), :] = src_ref[0, pl.ds(s, CONV_ROWS), :].astype(F32)
        return c

    lax.fori_loop(0, n_rows // CONV_ROWS, fill, 0)

    def step(i, c):
        s = pl.multiple_of(i * CONV_ROWS, CONV_ROWS)
        w = pad_ref[pl.ds(s, win), :]
        prev = pltpu.roll(w, 1, 0)[PAD:PAD + CONV_ROWS]
        nxt = pltpu.roll(w, win - 1, 0)[PAD:PAD + CONV_ROWS]
        cur = w[PAD:PAD + CONV_ROWS]
        y = prev * w_ref[0:1, :] + cur * w_ref[1:2, :] + nxt * w_ref[2:3, :] + b_ref[...]
        dst_ref[pl.ds(s, CONV_ROWS), :] = _silu(y)
        return c

    lax.fori_loop(0, n_rows // CONV_ROWS, step, 0)


def _tri_dot(tri, v):
    hi, mid, lo = _split3(v)
    return _dot(tri, hi) + _dot(tri, mid) + _dot(tri, lo)


def _expand(v, e_mat):
    hi, lo = _split2(v)
    return _dot(hi, e_mat) + _dot(lo, e_mat)


def _ssd_dir_step(xs, bmat, cmat, cums, dts, st_ref, e_mat, is_fwd):
    edge = cums[CHUNK - 1:CHUNK, :] if is_fwd else cums[0:1, :]
    wdec = jnp.exp(edge - cums) * dts
    xw = (xs * _expand(wdec, e_mat)).astype(BF16)
    s_loc = _dot(bmat.T.astype(BF16), xw)
    tot = _expand(jnp.broadcast_to(jnp.exp(edge), (8, LANES)), e_mat)[0:1, :]
    s_prev = st_ref[...]
    y_off = None
    if cmat is not None:
        y_off = _dot(cmat.astype(BF16), s_prev.astype(BF16)) * _expand(jnp.exp(cums), e_mat)
    st_ref[...] = s_prev * tot + s_loc
    return y_off


def _ssd_diag(xs, cb, cum, rc, dt):
    cum_t, rc_t, dt_t = cum.T, rc.T, dt.T
    ii = lax.broadcasted_iota(jnp.int32, (CHUNK, CHUNK), 0)
    jj = lax.broadcasted_iota(jnp.int32, (CHUNK, CHUNK), 1)
    lower, upper = ii >= jj, jj >= ii
    first_half = jj < SSD_HEAD_DIM
    outs = []
    for pair in range(HEADS_PER_GROUP // 2):
        ms = []
        for h in (2 * pair, 2 * pair + 1):
            hb = HEADS_PER_GROUP + h
            lf = jnp.exp(jnp.where(lower, cum[:, h:h + 1] - cum_t[h:h + 1, :], NEG_BIG)) * dt_t[h:h + 1, :]
            lb = jnp.exp(jnp.where(upper, rc[:, hb:hb + 1] - rc_t[hb:hb + 1, :], NEG_BIG)) * dt_t[hb:hb + 1, :]
            ms.append((cb * (lf + lb)).astype(BF16))
        xp = xs[:, pair * LANES:(pair + 1) * LANES]
        rhs = jnp.concatenate([jnp.where(first_half, xp, 0.0), jnp.where(first_half, 0.0, xp)], axis=0)
        outs.append(_dot(jnp.concatenate(ms, axis=1), rhs.astype(BF16)))
    return jnp.concatenate(outs, axis=1)


def _ssd_body(xl_ref, bl_ref, cl_ref, zl_ref, dtl_ref, xc_ref, bc_ref, dtc_ref,
              cwx_ref, cwb_ref, cwc_ref, cbx_ref, cbb_ref, cbc_ref,
              dtbias_ref, alog_ref, dskip_ref, ng_ref,
              o_ref,
              xpad, bpad, xs_scr, bs_scr, cs_scr, a_scr, dt_scr,
              xcpad, bcpad, xsc_scr, bsc_scr, ac_scr, dtcs_scr, y_scr, yb_scr, stf_scr, stb_scr):
    t_lat = xs_scr.shape[0]
    t_ctx = xsc_scr.shape[0]
    n_lat = t_lat // CHUNK
    n_ctx = t_ctx // CHUNK

    _conv3_silu(xl_ref, xpad, cwx_ref, cbx_ref, xs_scr, t_lat)
    _conv3_silu(bl_ref, bpad, cwb_ref, cbb_ref, bs_scr, t_lat)
    _conv3_silu(cl_ref, bpad, cwc_ref, cbc_ref, cs_scr, t_lat)
    _conv3_silu(xc_ref, xcpad, cwx_ref, cbx_ref, xsc_scr, t_ctx)
    _conv3_silu(bc_ref, bcpad, cwb_ref, cbb_ref, bsc_scr, t_ctx)
    neg_a = -jnp.exp(alog_ref[...])
    dt = jax.nn.softplus(dtl_ref[0] + dtbias_ref[...])
    dt_scr[...] = dt
    a_scr[...] = dt * neg_a
    dtc = jax.nn.softplus(dtc_ref[0] + dtbias_ref[...])
    dtcs_scr[...] = dtc
    ac_scr[...] = dtc * neg_a

    ii = lax.broadcasted_iota(jnp.int32, (CHUNK, CHUNK), 0)
    jj = lax.broadcasted_iota(jnp.int32, (CHUNK, CHUNK), 1)
    tril = (ii >= jj).astype(BF16)
    triu = (jj >= ii).astype(BF16)
    er = lax.broadcasted_iota(jnp.int32, (LANES, D_GROUP), 0)
    eh = lax.broadcasted_iota(jnp.int32, (LANES, D_GROUP), 1) // SSD_HEAD_DIM
    e_f = (er == eh).astype(BF16)
    e_b = (er == eh + HEADS_PER_GROUP).astype(BF16)

    def ctx_step(c, is_fwd):
        s = pl.multiple_of(c * CHUNK, CHUNK)
        cums = _tri_dot(tril if is_fwd else triu, ac_scr[pl.ds(s, CHUNK), :])
        _ssd_dir_step(xsc_scr[pl.ds(s, CHUNK), :], bsc_scr[pl.ds(s, CHUNK), :], None, cums,
                      dtcs_scr[pl.ds(s, CHUNK), :], stf_scr if is_fwd else stb_scr, e_f if is_fwd else e_b, is_fwd)

    def lat_step(c, carry):
        s = pl.multiple_of(c * CHUNK, CHUNK)
        a_blk = a_scr[pl.ds(s, CHUNK), :]
        dt_blk = dt_scr[pl.ds(s, CHUNK), :]
        cum = _tri_dot(tril, a_blk)
        rc = _tri_dot(triu, a_blk)
        xs = xs_scr[pl.ds(s, CHUNK), :]
        bm = bs_scr[pl.ds(s, CHUNK), :]
        cm = cs_scr[pl.ds(s, CHUNK), :]
        cb = lax.dot_general(cm.astype(BF16), bm.astype(BF16), (((1,), (1,)), ((), ())),
                             preferred_element_type=F32)
        y_diag = _ssd_diag(xs, cb, cum, rc, dt_blk)
        y_off = _ssd_dir_step(xs, bm, cm, cum, dt_blk, stf_scr, e_f, True)
        y_scr[pl.ds(s, CHUNK), :] = y_diag + y_off

        sb = pl.multiple_of((n_lat - 1 - c) * CHUNK, CHUNK)
        rcb = _tri_dot(triu, a_scr[pl.ds(sb, CHUNK), :])
        yb_scr[pl.ds(sb, CHUNK), :] = _ssd_dir_step(
            xs_scr[pl.ds(sb, CHUNK), :], bs_scr[pl.ds(sb, CHUNK), :], cs_scr[pl.ds(sb, CHUNK), :], rcb,
            dt_scr[pl.ds(sb, CHUNK), :], stb_scr, e_b, False)
        return carry

    stf_scr[...] = jnp.zeros(stf_scr.shape, F32)
    stb_scr[...] = jnp.zeros(stb_scr.shape, F32)
    for c in range(n_ctx):
        ctx_step(c, True)
    for c in reversed(range(n_ctx)):
        ctx_step(c, False)
    lax.fori_loop(0, n_lat, lat_step, 0)

    def finish(i, carry):
        s = pl.multiple_of(i * CONV_ROWS, CONV_ROWS)
        y = y_scr[pl.ds(s, CONV_ROWS), :] + yb_scr[pl.ds(s, CONV_ROWS), :]
        y = y + dskip_ref[...] * xs_scr[pl.ds(s, CONV_ROWS), :]
        y = y * _silu(zl_ref[0, pl.ds(s, CONV_ROWS), :].astype(F32))
        ms = jnp.mean(y * y, axis=-1, keepdims=True)
        o_ref[0, pl.ds(s, CONV_ROWS), :] = (y * lax.rsqrt(ms + EPS) * ng_ref[...]).astype(o_ref.dtype)
        return carry

    lax.fori_loop(0, t_lat // CONV_ROWS, finish, 0)


def _ssd(proj3, dt3, projc3, dtc3, conv_w, conv_b, dtbias_pk, alog_pk, dskip_row, norm_g_row):
    bsz, t_lat, _ = proj3.shape
    t_ctx = projc3.shape[1]
    xb, bb, cb_ = PK_X // D_GROUP, PK_B // D_STATE, PK_C // D_STATE
    cxb, cbb = 0, D_SSD // D_STATE
    wx0, wb0, wc0 = 0, D_SSD // D_STATE, (D_SSD + D_BC) // D_STATE
    in_specs = [
        pl.BlockSpec((1, t_lat, D_GROUP), lambda b, g: (b, 0, xb + g)),
        pl.BlockSpec((1, t_lat, D_STATE), lambda b, g: (b, 0, bb + g)),
        pl.BlockSpec((1, t_lat, D_STATE), lambda b, g: (b, 0, cb_ + g)),
        pl.BlockSpec((1, t_lat, D_GROUP), lambda b, g: (b, 0, g)),
        pl.BlockSpec((1, t_lat, LANES), lambda b, g: (b, 0, g)),
        pl.BlockSpec((1, t_ctx, D_GROUP), lambda b, g: (b, 0, cxb + g)),
        pl.BlockSpec((1, t_ctx, D_STATE), lambda b, g: (b, 0, cbb + g)),
        pl.BlockSpec((1, t_ctx, LANES), lambda b, g: (b, 0, g)),
        pl.BlockSpec((3, D_GROUP), lambda b, g: (0, wx0 + g)),
        pl.BlockSpec((3, D_STATE), lambda b, g: (0, wb0 + g)),
        pl.BlockSpec((3, D_STATE), lambda b, g: (0, wc0 + g)),
        pl.BlockSpec((1, D_GROUP), lambda b, g: (0, wx0 + g)),
        pl.BlockSpec((1, D_STATE), lambda b, g: (0, wb0 + g)),
        pl.BlockSpec((1, D_STATE), lambda b, g: (0, wc0 + g)),
        pl.BlockSpec((1, LANES), lambda b, g: (0, g)),
        pl.BlockSpec((1, LANES), lambda b, g: (0, g)),
        pl.BlockSpec((1, D_GROUP), lambda b, g: (0, g)),
        pl.BlockSpec((1, D_GROUP), lambda b, g: (0, g)),
    ]
    scratch = [
        pltpu.VMEM((t_lat + 2 * PAD, D_GROUP), F32),
        pltpu.VMEM((t_lat + 2 * PAD, D_STATE), F32),
        pltpu.VMEM((t_lat, D_GROUP), F32),
        pltpu.VMEM((t_lat, D_STATE), F32),
        pltpu.VMEM((t_lat, D_STATE), F32),
        pltpu.VMEM((t_lat, LANES), F32),
        pltpu.VMEM((t_lat, LANES), F32),
        pltpu.VMEM((t_ctx + 2 * PAD, D_GROUP), F32),
        pltpu.VMEM((t_ctx + 2 * PAD, D_STATE), F32),
        pltpu.VMEM((t_ctx, D_GROUP), F32),
        pltpu.VMEM((t_ctx, D_STATE), F32),
        pltpu.VMEM((t_ctx, LANES), F32),
        pltpu.VMEM((t_ctx, LANES), F32),
        pltpu.VMEM((t_lat, D_GROUP), F32),
        pltpu.VMEM((t_lat, D_GROUP), F32),
        pltpu.VMEM((D_STATE, D_GROUP), F32),
        pltpu.VMEM((D_STATE, D_GROUP), F32),
    ]
    return pl.pallas_call(
        _ssd_body,
        grid=(bsz, SSD_GROUPS),
        in_specs=in_specs,
        out_specs=pl.BlockSpec((1, t_lat, D_GROUP), lambda b, g: (b, 0, g)),
        out_shape=jax.ShapeDtypeStruct((bsz, t_lat, D_SSD), BF16),
        scratch_shapes=scratch,
        compiler_params=_cparams(("arbitrary", "arbitrary")),
        name="ssd_bidir",
    )(proj3, proj3, proj3, proj3, dt3, projc3, projc3, dtc3,
      conv_w, conv_w, conv_w, conv_b, conv_b, conv_b, dtbias_pk, alog_pk, dskip_row, norm_g_row)


def _conf_body(a_ref, g_ref, w_ref, b_ref, o_ref, v_scr, p_scr, c_scr, *, along_w):
    t = v_scr.shape[0]
    rows = t // GRID_W
    minor = rows if along_w else GRID_W
    halo = CONF_HALF * minor
    tc = v_scr.shape[1]

    def glu(i, c):
        s = pl.multiple_of(i * CONV_ROWS, CONV_ROWS)
        a = a_ref[0, pl.ds(s, CONV_ROWS), :].astype(F32)
        g = g_ref[0, pl.ds(s, CONV_ROWS), :].astype(F32)
        dst = v_scr if along_w else p_scr
        off = 0 if along_w else halo
        dst[pl.ds(s + off, CONV_ROWS), :] = a * jax.nn.sigmoid(g)
        return c

    lax.fori_loop(0, t // CONV_ROWS, glu, 0)
    p_scr[0:halo, :] = jnp.zeros((halo, tc), F32)
    p_scr[halo + t:halo + t + halo, :] = jnp.zeros((halo, tc), F32)
    if along_w:
        for w in range(GRID_W):
            p_scr[halo + w * rows:halo + (w + 1) * rows, :] = v_scr[pl.ds(w, rows, stride=GRID_W), :]

    def conv(i, c):
        s = pl.multiple_of(i * CONF_ROWS, CONF_ROWS)
        acc = jnp.broadcast_to(b_ref[...], (CONF_ROWS, tc))
        for k in range(CONF_KERNEL):
            acc = acc + w_ref[k:k + 1, :] * p_scr[pl.ds(s + k * minor, CONF_ROWS), :]
        if along_w:
            c_scr[pl.ds(s, CONF_ROWS), :] = acc
        else:
            o_ref[0, pl.ds(s, CONF_ROWS), :] = acc.astype(o_ref.dtype)
        return c

    lax.fori_loop(0, t // CONF_ROWS, conv, 0)
    if along_w:
        for r in range(rows):
            o_ref[0, r * GRID_W:(r + 1) * GRID_W, :] = c_scr[pl.ds(r, GRID_W, stride=rows), :].astype(o_ref.dtype)


def _conf_conv(proj3, dw_w, dw_b_row, along_w):
    bsz, t, _ = proj3.shape
    rows = t // GRID_W
    half = D_CONF // 2
    c0 = 0 if along_w else half
    a0, g0, w0 = (PK_CA + c0) // CONF_TC, (PK_CG + c0) // CONF_TC, c0 // CONF_TC
    halo = CONF_HALF * (rows if along_w else GRID_W)
    return pl.pallas_call(
        functools.partial(_conf_body, along_w=along_w),
        grid=(bsz, half // CONF_TC),
        in_specs=[
            pl.BlockSpec((1, t, CONF_TC), lambda b, j: (b, 0, a0 + j)),
            pl.BlockSpec((1, t, CONF_TC), lambda b, j: (b, 0, g0 + j)),
            pl.BlockSpec((CONF_KERNEL, CONF_TC), lambda b, j: (0, w0 + j)),
            pl.BlockSpec((1, CONF_TC), lambda b, j: (0, w0 + j)),
        ],
        out_specs=pl.BlockSpec((1, t, CONF_TC), lambda b, j: (b, 0, j)),
        out_shape=jax.ShapeDtypeStruct((bsz, t, half), BF16),
        scratch_shapes=[
            pltpu.VMEM((t, CONF_TC), F32),
            pltpu.VMEM((t + 2 * halo, CONF_TC), F32),
            pltpu.VMEM((t, CONF_TC), F32),
        ],
        compiler_params=_cparams(("arbitrary", "arbitrary")),
        name="conf_conv_w" if along_w else "conf_conv_r",
    )(proj3, proj3, dw_w, dw_b_row)


def _outproj_body(ssd_ref, cw_ref, cr_ref, x_ref, wt_ref, wb_ref, lng_ref, lnb_ref, gm_ref,
                  ng_ref, shf_ref, scf_ref, wrh_ref, wrl_ref, br_ref,
                  x1_ref, h2_ref, route_ref, cnt_ref, cnt_scr):
    i = pl.program_id(0)
    tm = x_ref.shape[0]

    @pl.when(i == 0)
    def _():
        cnt_scr[...] = jnp.zeros(cnt_scr.shape, F32)

    v = jnp.concatenate([cw_ref[...].astype(F32), cr_ref[...].astype(F32)], axis=1)
    mu = jnp.mean(v, axis=-1, keepdims=True)
    vc = v - mu
    var = jnp.mean(vc * vc, axis=-1, keepdims=True)
    cf = _silu(vc * lax.rsqrt(var + EPS) * lng_ref[...] + lnb_ref[...])
    acc = _dot(ssd_ref[...], wt_ref[...]) + _dot(cf.astype(BF16), wb_ref[...])
    x1 = x_ref[...] + gm_ref[0] * acc
    x1_ref[...] = x1

    ms = jnp.mean(x1 * x1, axis=-1, keepdims=True)
    h2 = x1 * lax.rsqrt(ms + EPS) * ng_ref[...] * (1.0 + scf_ref[0]) + shf_ref[0]
    h2_ref[...] = h2

    hh, hl = _split2(h2)
    logit = _dot(hh, wrh_ref[...]) + _dot(hl, wrh_ref[...]) + _dot(hh, wrl_ref[...]) + br_ref[...]
    lane = lax.broadcasted_iota(jnp.int32, (tm, LANES), 1).astype(F32)
    tops, hots, idxs = [], [], []
    for _ in range(TOP_K):
        m = jnp.max(logit, axis=-1, keepdims=True)
        idx = jnp.min(jnp.where(logit == m, lane, float(LANES)), axis=-1, keepdims=True)
        hot = lane == idx
        tops.append(m)
        hots.append(hot)
        idxs.append(idx)
        logit = jnp.where(hot, -jnp.inf, logit)
    es = [jnp.exp(m - tops[0]) for m in tops]
    den = es[0] + es[1] + es[2] + es[3]

    sel = jnp.zeros((tm, LANES), F32)
    for hot in hots:
        sel = sel + hot.astype(F32)
    ii = lax.broadcasted_iota(jnp.int32, (tm, tm), 0)
    jj = lax.broadcasted_iota(jnp.int32, (tm, tm), 1)
    before = _dot((ii > jj).astype(BF16), sel.astype(BF16)) + cnt_scr[0:1, :]
    packed = jnp.zeros((tm, LANES), F32)
    for k in range(TOP_K):
        pos = jnp.sum(jnp.where(hots[k], before, 0.0), axis=-1, keepdims=True)
        packed = packed + jnp.where(lane == float(k), idxs[k], 0.0)
        packed = packed + jnp.where(lane == float(TOP_K + k), es[k] / den, 0.0)
        packed = packed + jnp.where(lane == float(2 * TOP_K + k), pos, 0.0)
    route_ref[...] = packed
    cnt_scr[...] = cnt_scr[...] + jnp.sum(sel, axis=0, keepdims=True)
    cnt_ref[...] = cnt_scr[...]


def _out_proj(ssd2, confw2, confr2, x2, w_out_bf, ln_g, ln_b, g_m, ffn_g, sh_f, sc_f, wr_hi, wr_lo, br_pad,
              rows_per_mod):
    m = x2.shape[0]
    tm = OUT_TM
    per = rows_per_mod // tm
    half = D_CONF // 2
    row = lambda i: (i, 0)
    fixed = lambda i: (0, 0)
    mod = lambda i: (i // per, 0, 0)
    return pl.pallas_call(
        _outproj_body,
        grid=(m // tm,),
        in_specs=[
            pl.BlockSpec((tm, D_SSD), row),
            pl.BlockSpec((tm, half), row),
            pl.BlockSpec((tm, half), row),
            pl.BlockSpec((tm, D_MODEL), row),
            pl.BlockSpec((D_SSD, D_MODEL), lambda i: (0, 0), pipeline_mode=pl.Buffered(1)),
            pl.BlockSpec((D_CONF, D_MODEL), lambda i: (1, 0), pipeline_mode=pl.Buffered(1)),
            pl.BlockSpec((1, D_CONF), fixed),
            pl.BlockSpec((1, D_CONF), fixed),
            pl.BlockSpec((1, 1, D_MODEL), mod),
            pl.BlockSpec((1, D_MODEL), fixed),
            pl.BlockSpec((1, 1, D_MODEL), mod),
            pl.BlockSpec((1, 1, D_MODEL), mod),
            pl.BlockSpec((D_MODEL, LANES), fixed),
            pl.BlockSpec((D_MODEL, LANES), fixed),
            pl.BlockSpec((1, LANES), fixed),
        ],
        out_specs=[
            pl.BlockSpec((tm, D_MODEL), row),
            pl.BlockSpec((tm, D_MODEL), row),
            pl.BlockSpec((tm, LANES), row),
            pl.BlockSpec((8, LANES), fixed),
        ],
        out_shape=[
            jax.ShapeDtypeStruct((m, D_MODEL), F32),
            jax.ShapeDtypeStruct((m, D_MODEL), F32),
            jax.ShapeDtypeStruct((m, LANES), F32),
            jax.ShapeDtypeStruct((8, LANES), F32),
        ],
        scratch_shapes=[pltpu.VMEM((8, LANES), F32)],
        compiler_params=_cparams(("arbitrary",)),
        name="out_proj_route",
    )(ssd2, confw2, confr2, x2, w_out_bf, w_out_bf, ln_g, ln_b, g_m, ffn_g, sh_f, sc_f, wr_hi, wr_lo, br_pad)


def _scatter_body(fill0_ref, nfill_ref, dest_ref, h_ref, o_ref, zero_scr, sem):
    tm = h_ref.shape[0]

    def slot_copy(r, k):
        return pltpu.make_async_copy(h_ref.at[pl.ds(r, 1)], o_ref.at[pl.ds(dest_ref[0, 0, r * TOP_K + k], 1)], sem)

    def fill_copy(row):
        return pltpu.make_async_copy(zero_scr.at[pl.ds(0, 1)], o_ref.at[pl.ds(row, 1)], sem)

    def each_fill(fn):
        def per_expert(e, c):
            lax.fori_loop(0, nfill_ref[e], lambda j, cc: fn(fill0_ref[e] + j) or cc, 0)
            return c

        lax.fori_loop(0, N_EXPERTS, per_expert, 0)

    @pl.when(pl.program_id(0) == 0)
    def _():
        zero_scr[...] = jnp.zeros(zero_scr.shape, zero_scr.dtype)
        each_fill(lambda row: fill_copy(row).start())
        each_fill(lambda row: fill_copy(row).wait())

    def issue(r, c):
        for k in range(TOP_K):
            slot_copy(r, k).start()
        return c

    lax.fori_loop(0, tm, issue, 0)

    def drain(r, c):
        for k in range(TOP_K):
            slot_copy(r, k).wait()
        return c

    lax.fori_loop(0, tm, drain, 0)


def _dispatch(h2, dest3, fill0, nfill, n_rows):
    m, width = h2.shape
    tm = COMB_TM
    return pl.pallas_call(
        _scatter_body,
        grid_spec=pltpu.PrefetchScalarGridSpec(
            num_scalar_prefetch=2,
            grid=(m // tm,),
            in_specs=[
                pl.BlockSpec((1, 1, tm * TOP_K), lambda i, f0, nf: (i, 0, 0), memory_space=pltpu.SMEM),
                pl.BlockSpec((tm, width), lambda i, f0, nf: (i, 0)),
            ],
            out_specs=pl.BlockSpec(memory_space=pl.ANY),
            scratch_shapes=[pltpu.VMEM((8, width), h2.dtype), pltpu.SemaphoreType.DMA(())],
        ),
        out_shape=jax.ShapeDtypeStruct((n_rows, width), h2.dtype),
        compiler_params=_cparams(("arbitrary",)),
        name="moe_dispatch",
    )(fill0, nfill, dest3, h2)


def _moe_body(be_ref, nsub_ref, bi_ref, x_ref, wg_ref, wl_ref, bg_ref, bl_ref, wd_ref, bd_ref, o_ref,
              wg_scr, wl_scr, wd_scr):
    i = pl.program_id(0)
    f = pl.program_id(1)
    ns = nsub_ref[i]

    @pl.when(ns > 0)
    def _():
        @pl.when(f == 0)
        def _():
            o_ref[...] = jnp.broadcast_to(bd_ref[0], o_ref.shape)

        wg_scr[...] = wg_ref[0].astype(BF16)
        wl_scr[...] = wl_ref[0].astype(BF16)
        wd_scr[...] = wd_ref[0].astype(BF16)

        def sub(s, c):
            r0 = pl.multiple_of(s * MOE_SUB, MOE_SUB)
            xb = x_ref[pl.ds(r0, MOE_SUB), :].astype(BF16)
            gate = _dot(xb, wg_scr[...]) + bg_ref[0]
            lin = _dot(xb, wl_scr[...]) + bl_ref[0]
            glu = jnp.minimum(gate, SWIGLU_LIMIT)
            lin = jnp.clip(lin, -SWIGLU_LIMIT, SWIGLU_LIMIT)
            act = glu * jax.nn.sigmoid(SWIGLU_ALPHA * glu) * (lin + 1.0)
            o_ref[pl.ds(r0, MOE_SUB), :] += _dot(act.astype(BF16), wd_scr[...])
            return c

        lax.fori_loop(0, ns, sub, 0)


def _moe_mlp(xs, block_e, nsub, block_idx, w_gate_up, b_gate_up3, w_down, b_down3, n_super):
    n_ff = D_FF // MOE_TF
    lin0 = D_FF // MOE_TF

    def ff(i, f, ns):
        return jnp.where(ns[i] > 0, f, n_ff - 1)

    return pl.pallas_call(
        _moe_body,
        grid_spec=pltpu.PrefetchScalarGridSpec(
            num_scalar_prefetch=3,
            grid=(n_super, n_ff),
            in_specs=[
                pl.BlockSpec((MOE_SUPER, D_MODEL), lambda i, f, be, ns, bi: (bi[i], 0)),
                pl.BlockSpec((1, D_MODEL, MOE_TF), lambda i, f, be, ns, bi: (be[i], 0, ff(i, f, ns))),
                pl.BlockSpec((1, D_MODEL, MOE_TF), lambda i, f, be, ns, bi: (be[i], 0, lin0 + ff(i, f, ns))),
                pl.BlockSpec((1, 1, MOE_TF), lambda i, f, be, ns, bi: (be[i], 0, ff(i, f, ns))),
                pl.BlockSpec((1, 1, MOE_TF), lambda i, f, be, ns, bi: (be[i], 0, lin0 + ff(i, f, ns))),
                pl.BlockSpec((1, MOE_TF, D_MODEL), lambda i, f, be, ns, bi: (be[i], ff(i, f, ns), 0)),
                pl.BlockSpec((1, 1, D_MODEL), lambda i, f, be, ns, bi: (be[i], 0, 0)),
            ],
            out_specs=pl.BlockSpec((MOE_SUPER, D_MODEL), lambda i, f, be, ns, bi: (bi[i], 0)),
            scratch_shapes=[
                pltpu.VMEM((D_MODEL, MOE_TF), BF16),
                pltpu.VMEM((D_MODEL, MOE_TF), BF16),
                pltpu.VMEM((MOE_TF, D_MODEL), BF16),
            ],
        ),
        out_shape=jax.ShapeDtypeStruct((n_super * MOE_SUPER, D_MODEL), F32),
        compiler_params=_cparams(("arbitrary", "arbitrary")),
        name="moe_mlp",
    )(block_e, nsub, block_idx, xs, w_gate_up, w_gate_up, b_gate_up3, b_gate_up3, w_down, b_down3)


def _combine_body(dest_ref, y_ref, x1_ref, route_ref, gf_ref, fg_ref, o_ref, buf, sem):
    tm = x1_ref.shape[0]

    def issue(r, c):
        for k in range(TOP_K):
            pltpu.make_async_copy(y_ref.at[pl.ds(dest_ref[0, 0, r * TOP_K + k], 1)], buf.at[k, pl.ds(r, 1)], sem).start()
        return c

    lax.fori_loop(0, tm, issue, 0)

    def drain(r, c):
        for k in range(TOP_K):
            pltpu.make_async_copy(y_ref.at[pl.ds(0, 1)], buf.at[k, pl.ds(0, 1)], sem).wait()
        return c

    lax.fori_loop(0, tm, drain, 0)

    route = route_ref[...]
    moe = jnp.zeros(x1_ref.shape, F32)
    for k in range(TOP_K):
        moe = moe + route[:, TOP_K + k:TOP_K + k + 1] * buf[k]
    x2 = x1_ref[...] + gf_ref[0] * moe
    ms = jnp.mean(x2 * x2, axis=-1, keepdims=True)
    o_ref[...] = x2 * lax.rsqrt(ms + EPS) * fg_ref[...]


def _combine(dest3, y, x1, route, g_f, final_g, rows_per_mod):
    m = x1.shape[0]
    tm = COMB_TM
    per = rows_per_mod // tm
    return pl.pallas_call(
        _combine_body,
        grid=(m // tm,),
        in_specs=[
            pl.BlockSpec((1, 1, tm * TOP_K), lambda i: (i, 0, 0), memory_space=pltpu.SMEM),
            pl.BlockSpec(memory_space=pl.ANY),
            pl.BlockSpec((tm, D_MODEL), lambda i: (i, 0)),
            pl.BlockSpec((tm, LANES), lambda i: (i, 0)),
            pl.BlockSpec((1, 1, D_MODEL), lambda i: (i // per, 0, 0)),
            pl.BlockSpec((1, D_MODEL), lambda i: (0, 0)),
        ],
        out_specs=pl.BlockSpec((tm, D_MODEL), lambda i: (i, 0)),
        out_shape=jax.ShapeDtypeStruct((m, D_MODEL), F32),
        scratch_shapes=[pltpu.VMEM((TOP_K, tm, D_MODEL), F32), pltpu.SemaphoreType.DMA(())],
        compiler_params=_cparams(("arbitrary",)),
        name="moe_combine_norm",
    )(dest3, y, x1, route, g_f, final_g)


def _pack_heads(fwd, bwd):
    f = fwd.reshape(SSD_GROUPS, HEADS_PER_GROUP)
    b = bwd.reshape(SSD_GROUPS, HEADS_PER_GROUP)
    z = jnp.zeros((SSD_GROUPS, LANES - 2 * HEADS_PER_GROUP), F32)
    return jnp.concatenate([f, b, z], axis=1).reshape(1, SSD_GROUPS * LANES)


def _routing_tables(route, cnt, n_super):
    idx = route[:, 0:TOP_K].astype(jnp.int32)
    pos = route[:, 2 * TOP_K:3 * TOP_K].astype(jnp.int32)
    counts = cnt[0, :N_EXPERTS].astype(jnp.int32)
    padded = (counts + MOE_SUPER - 1) // MOE_SUPER * MOE_SUPER
    pad_end = jnp.cumsum(padded)
    pad_start = pad_end - padded
    dest = pad_start[idx] + pos
    n_live = pad_end[-1] // MOE_SUPER
    block_idx = jnp.minimum(jnp.arange(n_super, dtype=jnp.int32), n_live - 1)
    blk_start = block_idx * MOE_SUPER
    block_e = jnp.sum((pad_end[None, :] <= blk_start[:, None]).astype(jnp.int32), axis=1)
    block_e = jnp.minimum(block_e, N_EXPERTS - 1)
    live_rows = jnp.clip(counts[block_e] - (blk_start - pad_start[block_e]), 0, MOE_SUPER)
    nsub = (live_rows + MOE_SUB - 1) // MOE_SUB
    nsub = jnp.where(jnp.arange(n_super) < n_live, nsub, 0).astype(jnp.int32)
    fill0 = (pad_start + counts).astype(jnp.int32)
    nfill = ((counts + MOE_SUB - 1) // MOE_SUB * MOE_SUB - counts).astype(jnp.int32)
    return dest, block_e, nsub, block_idx, fill0, nfill


def kernel(x, c, ctx, c_ctx, w_mod, b_mod, norm_mix_g, norm_ffn_g, w_in, ssd_conv_w, ssd_conv_b, ssd_a_log_f, ssd_dt_bias_f, ssd_a_log_b, ssd_dt_bias_b, ssd_d_skip, ssd_norm_g, conf_dw_w, conf_dw_b, conf_ln_g, conf_ln_b, w_out, w_router, b_router, w_gate_up, b_gate_up, w_down, b_down, final_norm_g):
    bsz, n_lat, _ = x.shape
    n_ctx = ctx.shape[1]
    n_tok = bsz * n_lat
    l = 0

    c8 = jnp.zeros((8, D_MODEL), F32).at[:bsz].set(c).at[bsz].set(c_ctx)
    mod = _adaln_mod(c8, w_mod[l], b_mod[l][None, :])
    sh_m, sc_m, g_m, sh_f, sc_f, g_f = [mod[:bsz, k * D_MODEL:(k + 1) * D_MODEL].reshape(bsz, 1, D_MODEL)
                                        for k in range(N_MOD)]
    shc = mod[bsz, 0:D_MODEL].reshape(1, 1, D_MODEL)
    scc = mod[bsz, D_MODEL:2 * D_MODEL].reshape(1, 1, D_MODEL)

    wi = w_in[l]
    w_main = jnp.concatenate([wi[:, :OFF_DT], wi[:, OFF_CONF:]], axis=1).astype(BF16)
    wf = wi[:, OFF_DT:OFF_DT + SSD_HEADS].reshape(D_MODEL, SSD_GROUPS, HEADS_PER_GROUP)
    wb = wi[:, OFF_DT + SSD_HEADS:OFF_CONF].reshape(D_MODEL, SSD_GROUPS, HEADS_PER_GROUP)
    wz = jnp.zeros((D_MODEL, SSD_GROUPS, LANES - 2 * HEADS_PER_GROUP), F32)
    w_dt = jnp.concatenate([wf, wb, wz], axis=2).reshape(D_MODEL, SSD_GROUPS * LANES).astype(BF16)

    gain_mix = norm_mix_g[l][None, :]
    proj, dt_raw = _in_proj(x.reshape(n_tok, D_MODEL), gain_mix, sh_m, sc_m, w_main, w_dt,
                            0, PK_N // PROJ_TN, 1024, n_lat)
    n_cb = (D_SSD + 2 * D_BC) // PROJ_TN
    projc, dtc_raw = _in_proj(ctx.reshape(bsz * n_ctx, D_MODEL), gain_mix, shc, scc, w_main, w_dt,
                              PK_X // PROJ_TN, n_cb, bsz * n_ctx, bsz * n_ctx)
    proj3 = proj.reshape(bsz, n_lat, PK_N)

    ssd_out = _ssd(proj3, dt_raw.reshape(bsz, n_lat, -1), projc.reshape(bsz, n_ctx, -1),
                   dtc_raw.reshape(bsz, n_ctx, -1), ssd_conv_w[l], ssd_conv_b[l][None, :],
                   _pack_heads(ssd_dt_bias_f[l], ssd_dt_bias_b[l]), _pack_heads(ssd_a_log_f[l], ssd_a_log_b[l]),
                   jnp.repeat(ssd_d_skip[l], SSD_HEAD_DIM)[None, :], ssd_norm_g[l][None, :])

    conf_w = _conf_conv(proj3, conf_dw_w[l], conf_dw_b[l][None, :], True)
    conf_r = _conf_conv(proj3, conf_dw_w[l], conf_dw_b[l][None, :], False)

    wr = jnp.zeros((D_MODEL, LANES), F32).at[:, :N_EXPERTS].set(w_router[l])
    wr_hi = wr.astype(BF16)
    wr_lo = (wr - wr_hi.astype(F32)).astype(BF16)
    br = jnp.full((1, LANES), NEG_BIG, F32).at[0, :N_EXPERTS].set(b_router[l])
    half = D_CONF // 2
    x1, h2p, route, cnt = _out_proj(
        ssd_out.reshape(n_tok, D_SSD), conf_w.reshape(n_tok, half), conf_r.reshape(n_tok, half),
        x.reshape(n_tok, D_MODEL), w_out[l].astype(BF16), conf_ln_g[l][None, :], conf_ln_b[l][None, :], g_m,
        norm_ffn_g[l][None, :], sh_f, sc_f, wr_hi, wr_lo, br, n_lat)

    n_super = n_tok * TOP_K // MOE_SUPER + N_EXPERTS
    dest, block_e, nsub, block_idx, fill0, nfill = _routing_tables(route, cnt, n_super)
    dest3 = dest.reshape(n_tok // COMB_TM, 1, COMB_TM * TOP_K)
    xs = _dispatch(h2p, dest3, fill0, nfill, n_super * MOE_SUPER)
    y = _moe_mlp(xs, block_e, nsub, block_idx, w_gate_up[l], b_gate_up[l][:, None, :], w_down[l], b_down[l][:, None, :], n_super)
    out = _combine(dest3, y, x1, route, g_f, final_norm_g[None, :], n_lat)
    return out.reshape(bsz, n_lat, D_MODEL)
```

```python
import functools

import jax
import jax.numpy as jnp
from jax import lax
from jax.experimental import pallas as pl
from jax.experimental.pallas import tpu as pltpu

F32 = jnp.float32
BF16 = jnp.bfloat16

D_MODEL = 2048
N_MOD = 6
EPS = 1e-6
GRID_W = 64

SSD_HEADS = 32
SSD_HEAD_DIM = 64
D_SSD = SSD_HEADS * SSD_HEAD_DIM
SSD_GROUPS = 4
HEADS_PER_GROUP = SSD_HEADS // SSD_GROUPS
D_GROUP = D_SSD // SSD_GROUPS
D_STATE = 128
CHUNK = 128
D_BC = SSD_GROUPS * D_STATE
D_CONF = 2048
CONF_KERNEL = 31
CONF_HALF = CONF_KERNEL // 2
N_EXPERTS = 32
TOP_K = 4
D_FF = D_MODEL
SWIGLU_LIMIT = 7.0
SWIGLU_ALPHA = 1.702

OFF_XBC = D_SSD
OFF_DT = OFF_XBC + D_SSD + 2 * D_BC
OFF_CONF = OFF_DT + 2 * SSD_HEADS

PK_Z = 0
PK_X = D_SSD
PK_B = PK_X + D_SSD
PK_C = PK_B + D_BC
PK_CA = PK_C + D_BC
PK_CG = PK_CA + D_CONF
PK_N = PK_CG + D_CONF

LANES = 128
VMEM_LIMIT_BYTES = 56 * 1024 * 1024
NEG_BIG = -1e30

PROJ_TN = 512
MOE_SUB = 384
MOE_SUPER = 3 * MOE_SUB
MOE_TF = 256
OUT_TM = 256
COMB_TM = 256
CONF_TC = LANES
CONF_ROWS = 128


def _cparams(sem):
    return pltpu.CompilerParams(dimension_semantics=sem, vmem_limit_bytes=VMEM_LIMIT_BYTES)


def _silu(v):
    return v * jax.nn.sigmoid(v)


def _dot(a, b):
    return jnp.dot(a, b, preferred_element_type=F32)


def _split2(v):
    hi = v.astype(BF16)
    lo = (v - hi.astype(F32)).astype(BF16)
    return hi, lo


def _split3(v):
    hi = v.astype(BF16)
    r1 = v - hi.astype(F32)
    mid = r1.astype(BF16)
    lo = (r1 - mid.astype(F32)).astype(BF16)
    return hi, mid, lo


def _mod_body(c_ref, w_ref, b_ref, o_ref):
    s = _silu(c_ref[...])
    o_ref[...] = _dot(s.astype(BF16), w_ref[...].astype(BF16)) + b_ref[...]


def _adaln_mod(c8, w_mod, b_mod):
    tn = 1024
    n = w_mod.shape[1]
    return pl.pallas_call(
        _mod_body,
        grid=(n // tn,),
        in_specs=[
            pl.BlockSpec((8, D_MODEL), lambda j: (0, 0)),
            pl.BlockSpec((D_MODEL, tn), lambda j: (0, j)),
            pl.BlockSpec((1, tn), lambda j: (0, j)),
        ],
        out_specs=pl.BlockSpec((8, tn), lambda j: (0, j)),
        out_shape=jax.ShapeDtypeStruct((8, n), F32),
        compiler_params=_cparams(("arbitrary",)),
        name="adaln_mod",
    )(c8, w_mod, b_mod)


def _inproj_body(x_ref, g_ref, sh_ref, sc_ref, wa_ref, wb_ref, wdt_ref, o_ref, odt_ref, h_scr, *, col_block0, n_a):
    @pl.when(pl.program_id(1) == 0)
    def _():
        xf = x_ref[...]
        ms = jnp.mean(xf * xf, axis=-1, keepdims=True)
        y = xf * lax.rsqrt(ms + EPS) * g_ref[...]
        hb = (y * (1.0 + sc_ref[0]) + sh_ref[0]).astype(BF16)
        h_scr[...] = hb
        odt_ref[...] = _dot(hb, wdt_ref[...])

    j = pl.program_id(1) + col_block0

    @pl.when(j < n_a)
    def _():
        o_ref[...] = _dot(h_scr[...], wa_ref[...]).astype(o_ref.dtype)

    @pl.when(j >= n_a)
    def _():
        o_ref[...] = _dot(h_scr[...], wb_ref[...]).astype(o_ref.dtype)


def _in_proj(x2d, gain, shift, scale, w_a, w_b, w_dt, col_block0, n_col_blocks, tm, rows_per_mod):
    m = x2d.shape[0]
    mods_per_tile = rows_per_mod // tm
    n_dt = w_dt.shape[1]
    n_a = w_a.shape[1] // PROJ_TN
    return pl.pallas_call(
        functools.partial(_inproj_body, col_block0=col_block0, n_a=n_a),
        grid=(m // tm, n_col_blocks),
        in_specs=[
            pl.BlockSpec((tm, D_MODEL), lambda i, j: (i, 0)),
            pl.BlockSpec((1, D_MODEL), lambda i, j: (0, 0)),
            pl.BlockSpec((1, 1, D_MODEL), lambda i, j: (i // mods_per_tile, 0, 0)),
            pl.BlockSpec((1, 1, D_MODEL), lambda i, j: (i // mods_per_tile, 0, 0)),
            pl.BlockSpec((D_MODEL, PROJ_TN), lambda i, j: (0, jnp.minimum(col_block0 + j, n_a - 1))),
            pl.BlockSpec((D_MODEL, PROJ_TN), lambda i, j: (0, jnp.maximum(col_block0 + j - n_a, 0))),
            pl.BlockSpec((D_MODEL, n_dt), lambda i, j: (0, 0)),
        ],
        out_specs=[
            pl.BlockSpec((tm, PROJ_TN), lambda i, j: (i, j)),
            pl.BlockSpec((tm, n_dt), lambda i, j: (i, 0)),
        ],
        out_shape=[
            jax.ShapeDtypeStruct((m, n_col_blocks * PROJ_TN), BF16),
            jax.ShapeDtypeStruct((m, n_dt), F32),
        ],
        scratch_shapes=[pltpu.VMEM((tm, D_MODEL), BF16)],
        compiler_params=_cparams(("arbitrary", "arbitrary")),
        name="in_proj",
    )(x2d, gain, shift, scale, w_a, w_b, w_dt)


CONV_ROWS = 256
PAD = 8


def _conv3_silu(src_ref, pad_ref, w_ref, b_ref, dst_ref, n_rows):
    ncol = pad_ref.shape[1]
    zero = jnp.zeros((PAD, ncol), F32)
    pad_ref[0:PAD, :] = zero
    pad_ref[PAD + n_rows:PAD + n_rows + PAD, :] = zero
    win = CONV_ROWS + 2 * PAD

    def fill(i, c):
        s = pl.multiple_of(i * CONV_ROWS, CONV_ROWS)
        pad_ref[pl.ds(s + PAD, CONV_ROWS), :] = src_ref[0, pl.ds(s, CONV_ROWS), :].astype(F32)
        return c

    lax.fori_loop(0, n_rows // CONV_ROWS, fill, 0)

    def step(i, c):
        s = pl.multiple_of(i * CONV_ROWS, CONV_ROWS)
        w = pad_ref[pl.ds(s, win), :]
        prev = pltpu.roll(w, 1, 0)[PAD:PAD + CONV_ROWS]
        nxt = pltpu.roll(w, win - 1, 0)[PAD:PAD + CONV_ROWS]
        cur = w[PAD:PAD + CONV_ROWS]
        y = prev * w_ref[0:1, :] + cur * w_ref[1:2, :] + nxt * w_ref[2:3, :] + b_ref[...]
        dst_ref[pl.ds(s, CONV_ROWS), :] = _silu(y)
        return c

    lax.fori_loop(0, n_rows // CONV_ROWS, step, 0)


def _tri_dot(tri, v):
    hi, mid, lo = _split3(v)
    return _dot(tri, hi) + _dot(tri, mid) + _dot(tri, lo)


def _expand(v, e_mat):
    hi, lo = _split2(v)
    return _dot(hi, e_mat) + _dot(lo, e_mat)


def _ssd_dir_step(xs, bmat, cmat, cums, dts, st_ref, e_mat, is_fwd):
    edge = cums[CHUNK - 1:CHUNK, :] if is_fwd else cums[0:1, :]
    wdec = jnp.exp(edge - cums) * dts
    xw = (xs * _expand(wdec, e_mat)).astype(BF16)
    s_loc = _dot(bmat.T.astype(BF16), xw)
    tot = _expand(jnp.broadcast_to(jnp.exp(edge), (8, LANES)), e_mat)[0:1, :]
    s_prev = st_ref[...]
    y_off = None
    if cmat is not None:
        y_off = _dot(cmat.astype(BF16), s_prev.astype(BF16)) * _expand(jnp.exp(cums), e_mat)
    st_ref[...] = s_prev * tot + s_loc
    return y_off


def _ssd_diag(xs, cb, cum, rc, dt):
    cum_t, rc_t, dt_t = cum.T, rc.T, dt.T
    ii = lax.broadcasted_iota(jnp.int32, (CHUNK, CHUNK), 0)
    jj = lax.broadcasted_iota(jnp.int32, (CHUNK, CHUNK), 1)
    lower, upper = ii >= jj, jj >= ii
    first_half = jj < SSD_HEAD_DIM
    outs = []
    for pair in range(HEADS_PER_GROUP // 2):
        ms = []
        for h in (2 * pair, 2 * pair + 1):
            hb = HEADS_PER_GROUP + h
            lf = jnp.exp(jnp.where(lower, cum[:, h:h + 1] - cum_t[h:h + 1, :], NEG_BIG)) * dt_t[h:h + 1, :]
            lb = jnp.exp(jnp.where(upper, rc[:, hb:hb + 1] - rc_t[hb:hb + 1, :], NEG_BIG)) * dt_t[hb:hb + 1, :]
            ms.append((cb * (lf + lb)).astype(BF16))
        xp = xs[:, pair * LANES:(pair + 1) * LANES]
        rhs = jnp.concatenate([jnp.where(first_half, xp, 0.0), jnp.where(first_half, 0.0, xp)], axis=0)
        outs.append(_dot(jnp.concatenate(ms, axis=1), rhs.astype(BF16)))
    return jnp.concatenate(outs, axis=1)


def _ssd_body(xl_ref, bl_ref, cl_ref, zl_ref, dtl_ref, xc_ref, bc_ref, dtc_ref,
              cwx_ref, cwb_ref, cwc_ref, cbx_ref, cbb_ref, cbc_ref,
              dtbias_ref, alog_ref, dskip_ref, ng_ref,
              o_ref,
              xpad, bpad, xs_scr, bs_scr, cs_scr, a_scr, dt_scr,
              xcpad, bcpad, xsc_scr, bsc_scr, ac_scr, dtcs_scr, y_scr, yb_scr, stf_scr, stb_scr):
    t_lat = xs_scr.shape[0]
    t_ctx = xsc_scr.shape[0]
    n_lat = t_lat // CHUNK
    n_ctx = t_ctx // CHUNK

    _conv3_silu(xl_ref, xpad, cwx_ref, cbx_ref, xs_scr, t_lat)
    _conv3_silu(bl_ref, bpad, cwb_ref, cbb_ref, bs_scr, t_lat)
    _conv3_silu(cl_ref, bpad, cwc_ref, cbc_ref, cs_scr, t_lat)
    _conv3_silu(xc_ref, xcpad, cwx_ref, cbx_ref, xsc_scr, t_ctx)
    _conv3_silu(bc_ref, bcpad, cwb_ref, cbb_ref, bsc_scr, t_ctx)
    neg_a = -jnp.exp(alog_ref[...])
    dt = jax.nn.softplus(dtl_ref[0] + dtbias_ref[...])
    dt_scr[...] = dt
    a_scr[...] = dt * neg_a
    dtc = jax.nn.softplus(dtc_ref[0] + dtbias_ref[...])
    dtcs_scr[...] = dtc
    ac_scr[...] = dtc * neg_a

    ii = lax.broadcasted_iota(jnp.int32, (CHUNK, CHUNK), 0)
    jj = lax.broadcasted_iota(jnp.int32, (CHUNK, CHUNK), 1)
    tril = (ii >= jj).astype(BF16)
    triu = (jj >= ii).astype(BF16)
    er = lax.broadcasted_iota(jnp.int32, (LANES, D_GROUP), 0)
    eh = lax.broadcasted_iota(jnp.int32, (LANES, D_GROUP), 1) // SSD_HEAD_DIM
    e_f = (er == eh).astype(BF16)
    e_b = (er == eh + HEADS_PER_GROUP).astype(BF16)

    def ctx_step(c, is_fwd):
        s = pl.multiple_of(c * CHUNK, CHUNK)
        cums = _tri_dot(tril if is_fwd else triu, ac_scr[pl.ds(s, CHUNK), :])
        _ssd_dir_step(xsc_scr[pl.ds(s, CHUNK), :], bsc_scr[pl.ds(s, CHUNK), :], None, cums,
                      dtcs_scr[pl.ds(s, CHUNK), :], stf_scr if is_fwd else stb_scr, e_f if is_fwd else e_b, is_fwd)

    def lat_step(c, carry):
        s = pl.multiple_of(c * CHUNK, CHUNK)
        a_blk = a_scr[pl.ds(s, CHUNK), :]
        dt_blk = dt_scr[pl.ds(s, CHUNK), :]
        cum = _tri_dot(tril, a_blk)
        rc = _tri_dot(triu, a_blk)
        xs = xs_scr[pl.ds(s, CHUNK), :]
        bm = bs_scr[pl.ds(s, CHUNK), :]
        cm = cs_scr[pl.ds(s, CHUNK), :]
        cb = lax.dot_general(cm.astype(BF16), bm.astype(BF16), (((1,), (1,)), ((), ())),
                             preferred_element_type=F32)
        y_diag = _ssd_diag(xs, cb, cum, rc, dt_blk)
        y_off = _ssd_dir_step(xs, bm, cm, cum, dt_blk, stf_scr, e_f, True)
        y_scr[pl.ds(s, CHUNK), :] = y_diag + y_off

        sb = pl.multiple_of((n_lat - 1 - c) * CHUNK, CHUNK)
        rcb = _tri_dot(triu, a_scr[pl.ds(sb, CHUNK), :])
        yb_scr[pl.ds(sb, CHUNK), :] = _ssd_dir_step(
            xs_scr[pl.ds(sb, CHUNK), :], bs_scr[pl.ds(sb, CHUNK), :], cs_scr[pl.ds(sb, CHUNK), :], rcb,
            dt_scr[pl.ds(sb, CHUNK), :], stb_scr, e_b, False)
        return carry

    stf_scr[...] = jnp.zeros(stf_scr.shape, F32)
    stb_scr[...] = jnp.zeros(stb_scr.shape, F32)
    for c in range(n_ctx):
        ctx_step(c, True)
    for c in reversed(range(n_ctx)):
        ctx_step(c, False)
    lax.fori_loop(0, n_lat, lat_step, 0)

    def finish(i, carry):
        s = pl.multiple_of(i * CONV_ROWS, CONV_ROWS)
        y = y_scr[pl.ds(s, CONV_ROWS), :] + yb_scr[pl.ds(s, CONV_ROWS), :]
        y = y + dskip_ref[...] * xs_scr[pl.ds(s, CONV_ROWS), :]
        y = y * _silu(zl_ref[0, pl.ds(s, CONV_ROWS), :].astype(F32))
        ms = jnp.mean(y * y, axis=-1, keepdims=True)
        o_ref[0, pl.ds(s, CONV_ROWS), :] = (y * lax.rsqrt(ms + EPS) * ng_ref[...]).astype(o_ref.dtype)
        return carry

    lax.fori_loop(0, t_lat // CONV_ROWS, finish, 0)


def _ssd(proj3, dt3, projc3, dtc3, conv_w, conv_b, dtbias_pk, alog_pk, dskip_row, norm_g_row):
    bsz, t_lat, _ = proj3.shape
    t_ctx = projc3.shape[1]
    xb, bb, cb_ = PK_X // D_GROUP, PK_B // D_STATE, PK_C // D_STATE
    cxb, cbb = 0, D_SSD // D_STATE
    wx0, wb0, wc0 = 0, D_SSD // D_STATE, (D_SSD + D_BC) // D_STATE
    in_specs = [
        pl.BlockSpec((1, t_lat, D_GROUP), lambda b, g: (b, 0, xb + g)),
        pl.BlockSpec((1, t_lat, D_STATE), lambda b, g: (b, 0, bb + g)),
        pl.BlockSpec((1, t_lat, D_STATE), lambda b, g: (b, 0, cb_ + g)),
        pl.BlockSpec((1, t_lat, D_GROUP), lambda b, g: (b, 0, g)),
        pl.BlockSpec((1, t_lat, LANES), lambda b, g: (b, 0, g)),
        pl.BlockSpec((1, t_ctx, D_GROUP), lambda b, g: (b, 0, cxb + g)),
        pl.BlockSpec((1, t_ctx, D_STATE), lambda b, g: (b, 0, cbb + g)),
        pl.BlockSpec((1, t_ctx, LANES), lambda b, g: (b, 0, g)),
        pl.BlockSpec((3, D_GROUP), lambda b, g: (0, wx0 + g)),
        pl.BlockSpec((3, D_STATE), lambda b, g: (0, wb0 + g)),
        pl.BlockSpec((3, D_STATE), lambda b, g: (0, wc0 + g)),
        pl.BlockSpec((1, D_GROUP), lambda b, g: (0, wx0 + g)),
        pl.BlockSpec((1, D_STATE), lambda b, g: (0, wb0 + g)),
        pl.BlockSpec((1, D_STATE), lambda b, g: (0, wc0 + g)),
        pl.BlockSpec((1, LANES), lambda b, g: (0, g)),
        pl.BlockSpec((1, LANES), lambda b, g: (0, g)),
        pl.BlockSpec((1, D_GROUP), lambda b, g: (0, g)),
        pl.BlockSpec((1, D_GROUP), lambda b, g: (0, g)),
    ]
    scratch = [
        pltpu.VMEM((t_lat + 2 * PAD, D_GROUP), F32),
        pltpu.VMEM((t_lat + 2 * PAD, D_STATE), F32),
        pltpu.VMEM((t_lat, D_GROUP), F32),
        pltpu.VMEM((t_lat, D_STATE), F32),
        pltpu.VMEM((t_lat, D_STATE), F32),
        pltpu.VMEM((t_lat, LANES), F32),
        pltpu.VMEM((t_lat, LANES), F32),
        pltpu.VMEM((t_ctx + 2 * PAD, D_GROUP), F32),
        pltpu.VMEM((t_ctx + 2 * PAD, D_STATE), F32),
        pltpu.VMEM((t_ctx, D_GROUP), F32),
        pltpu.VMEM((t_ctx, D_STATE), F32),
        pltpu.VMEM((t_ctx, LANES), F32),
        pltpu.VMEM((t_ctx, LANES), F32),
        pltpu.VMEM((t_lat, D_GROUP), F32),
        pltpu.VMEM((t_lat, D_GROUP), F32),
        pltpu.VMEM((D_STATE, D_GROUP), F32),
        pltpu.VMEM((D_STATE, D_GROUP), F32),
    ]
    return pl.pallas_call(
        _ssd_body,
        grid=(bsz, SSD_GROUPS),
        in_specs=in_specs,
        out_specs=pl.BlockSpec((1, t_lat, D_GROUP), lambda b, g: (b, 0, g)),
        out_shape=jax.ShapeDtypeStruct((bsz, t_lat, D_SSD), BF16),
        scratch_shapes=scratch,
        compiler_params=_cparams(("arbitrary", "arbitrary")),
        name="ssd_bidir",
    )(proj3, proj3, proj3, proj3, dt3, projc3, projc3, dtc3,
      conv_w, conv_w, conv_w, conv_b, conv_b, conv_b, dtbias_pk, alog_pk, dskip_row, norm_g_row)


def _conf_body(a_ref, g_ref, w_ref, b_ref, o_ref, v_scr, p_scr, c_scr, *, along_w):
    t = v_scr.shape[0]
    rows = t // GRID_W
    minor = rows if along_w else GRID_W
    halo = CONF_HALF * minor
    tc = v_scr.shape[1]

    def glu(i, c):
        s = pl.multiple_of(i * CONV_ROWS, CONV_ROWS)
        a = a_ref[0, pl.ds(s, CONV_ROWS), :].astype(F32)
        g = g_ref[0, pl.ds(s, CONV_ROWS), :].astype(F32)
        dst = v_scr if along_w else p_scr
        off = 0 if along_w else halo
        dst[pl.ds(s + off, CONV_ROWS), :] = a * jax.nn.sigmoid(g)
        return c

    lax.fori_loop(0, t // CONV_ROWS, glu, 0)
    p_scr[0:halo, :] = jnp.zeros((halo, tc), F32)
    p_scr[halo + t:halo + t + halo, :] = jnp.zeros((halo, tc), F32)
    if along_w:
        for w in range(GRID_W):
            p_scr[halo + w * rows:halo + (w + 1) * rows, :] = v_scr[pl.ds(w, rows, stride=GRID_W), :]

    def conv(i, c):
        s = pl.multiple_of(i * CONF_ROWS, CONF_ROWS)
        acc = jnp.broadcast_to(b_ref[...], (CONF_ROWS, tc))
        for k in range(CONF_KERNEL):
            acc = acc + w_ref[k:k + 1, :] * p_scr[pl.ds(s + k * minor, CONF_ROWS), :]
        if along_w:
            c_scr[pl.ds(s, CONF_ROWS), :] = acc
        else:
            o_ref[0, pl.ds(s, CONF_ROWS), :] = acc.astype(o_ref.dtype)
        return c

    lax.fori_loop(0, t // CONF_ROWS, conv, 0)
    if along_w:
        for r in range(rows):
            o_ref[0, r * GRID_W:(r + 1) * GRID_W, :] = c_scr[pl.ds(r, GRID_W, stride=rows), :].astype(o_ref.dtype)


def _conf_conv(proj3, dw_w, dw_b_row, along_w):
    bsz, t, _ = proj3.shape
    rows = t // GRID_W
    half = D_CONF // 2
    c0 = 0 if along_w else half
    a0, g0, w0 = (PK_CA + c0) // CONF_TC, (PK_CG + c0) // CONF_TC, c0 // CONF_TC
    halo = CONF_HALF * (rows if along_w else GRID_W)
    return pl.pallas_call(
        functools.partial(_conf_body, along_w=along_w),
        grid=(bsz, half // CONF_TC),
        in_specs=[
            pl.BlockSpec((1, t, CONF_TC), lambda b, j: (b, 0, a0 + j)),
            pl.BlockSpec((1, t, CONF_TC), lambda b, j: (b, 0, g0 + j)),
            pl.BlockSpec((CONF_KERNEL, CONF_TC), lambda b, j: (0, w0 + j)),
            pl.BlockSpec((1, CONF_TC), lambda b, j: (0, w0 + j)),
        ],
        out_specs=pl.BlockSpec((1, t, CONF_TC), lambda b, j: (b, 0, j)),
        out_shape=jax.ShapeDtypeStruct((bsz, t, half), BF16),
        scratch_shapes=[
            pltpu.VMEM((t, CONF_TC), F32),
            pltpu.VMEM((t + 2 * halo, CONF_TC), F32),
            pltpu.VMEM((t, CONF_TC), F32),
        ],
        compiler_params=_cparams(("arbitrary", "arbitrary")),
        name="conf_conv_w" if along_w else "conf_conv_r",
    )(proj3, proj3, dw_w, dw_b_row)


def _outproj_body(ssd_ref, cw_ref, cr_ref, x_ref, wt_ref, wb_ref, lng_ref, lnb_ref, gm_ref,
                  ng_ref, shf_ref, scf_ref, wrh_ref, wrl_ref, br_ref,
                  x1_ref, h2_ref, route_ref, cnt_ref, cnt_scr):
    i = pl.program_id(0)
    tm = x_ref.shape[0]

    @pl.when(i == 0)
    def _():
        cnt_scr[...] = jnp.zeros(cnt_scr.shape, F32)

    v = jnp.concatenate([cw_ref[...].astype(F32), cr_ref[...].astype(F32)], axis=1)
    mu = jnp.mean(v, axis=-1, keepdims=True)
    vc = v - mu
    var = jnp.mean(vc * vc, axis=-1, keepdims=True)
    cf = _silu(vc * lax.rsqrt(var + EPS) * lng_ref[...] + lnb_ref[...])
    acc = _dot(ssd_ref[...], wt_ref[...]) + _dot(cf.astype(BF16), wb_ref[...])
    x1 = x_ref[...] + gm_ref[0] * acc
    x1_ref[...] = x1

    ms = jnp.mean(x1 * x1, axis=-1, keepdims=True)
    h2 = x1 * lax.rsqrt(ms + EPS) * ng_ref[...] * (1.0 + scf_ref[0]) + shf_ref[0]
    h2_ref[...] = h2

    hh, hl = _split2(h2)
    logit = _dot(hh, wrh_ref[...]) + _dot(hl, wrh_ref[...]) + _dot(hh, wrl_ref[...]) + br_ref[...]
    lane = lax.broadcasted_iota(jnp.int32, (tm, LANES), 1).astype(F32)
    tops, hots, idxs = [], [], []
    for _ in range(TOP_K):
        m = jnp.max(logit, axis=-1, keepdims=True)
        idx = jnp.min(jnp.where(logit == m, lane, float(LANES)), axis=-1, keepdims=True)
        hot = lane == idx
        tops.append(m)
        hots.append(hot)
        idxs.append(idx)
        logit = jnp.where(hot, -jnp.inf, logit)
    es = [jnp.exp(m - tops[0]) for m in tops]
    den = es[0] + es[1] + es[2] + es[3]

    sel = jnp.zeros((tm, LANES), F32)
    for hot in hots:
        sel = sel + hot.astype(F32)
    ii = lax.broadcasted_iota(jnp.int32, (tm, tm), 0)
    jj = lax.broadcasted_iota(jnp.int32, (tm, tm), 1)
    before = _dot((ii > jj).astype(BF16), sel.astype(BF16)) + cnt_scr[0:1, :]
    packed = jnp.zeros((tm, LANES), F32)
    for k in range(TOP_K):
        pos = jnp.sum(jnp.where(hots[k], before, 0.0), axis=-1, keepdims=True)
        packed = packed + jnp.where(lane == float(k), idxs[k], 0.0)
        packed = packed + jnp.where(lane == float(TOP_K + k), es[k] / den, 0.0)
        packed = packed + jnp.where(lane == float(2 * TOP_K + k), pos, 0.0)
    route_ref[...] = packed
    cnt_scr[...] = cnt_scr[...] + jnp.sum(sel, axis=0, keepdims=True)
    cnt_ref[...] = cnt_scr[...]


def _out_proj(ssd2, confw2, confr2, x2, w_out_bf, ln_g, ln_b, g_m, ffn_g, sh_f, sc_f, wr_hi, wr_lo, br_pad,
              rows_per_mod):
    m = x2.shape[0]
    tm = OUT_TM
    per = rows_per_mod // tm
    half = D_CONF // 2
    row = lambda i: (i, 0)
    fixed = lambda i: (0, 0)
    mod = lambda i: (i // per, 0, 0)
    return pl.pallas_call(
        _outproj_body,
        grid=(m // tm,),
        in_specs=[
            pl.BlockSpec((tm, D_SSD), row),
            pl.BlockSpec((tm, half), row),
            pl.BlockSpec((tm, half), row),
            pl.BlockSpec((tm, D_MODEL), row),
            pl.BlockSpec((D_SSD, D_MODEL), lambda i: (0, 0), pipeline_mode=pl.Buffered(1)),
            pl.BlockSpec((D_CONF, D_MODEL), lambda i: (1, 0), pipeline_mode=pl.Buffered(1)),
            pl.BlockSpec((1, D_CONF), fixed),
            pl.BlockSpec((1, D_CONF), fixed),
            pl.BlockSpec((1, 1, D_MODEL), mod),
            pl.BlockSpec((1, D_MODEL), fixed),
            pl.BlockSpec((1, 1, D_MODEL), mod),
            pl.BlockSpec((1, 1, D_MODEL), mod),
            pl.BlockSpec((D_MODEL, LANES), fixed),
            pl.BlockSpec((D_MODEL, LANES), fixed),
            pl.BlockSpec((1, LANES), fixed),
        ],
        out_specs=[
            pl.BlockSpec((tm, D_MODEL), row),
            pl.BlockSpec((tm, D_MODEL), row),
            pl.BlockSpec((tm, LANES), row),
            pl.BlockSpec((8, LANES), fixed),
        ],
        out_shape=[
            jax.ShapeDtypeStruct((m, D_MODEL), F32),
            jax.ShapeDtypeStruct((m, D_MODEL), F32),
            jax.ShapeDtypeStruct((m, LANES), F32),
            jax.ShapeDtypeStruct((8, LANES), F32),
        ],
        scratch_shapes=[pltpu.VMEM((8, LANES), F32)],
        compiler_params=_cparams(("arbitrary",)),
        name="out_proj_route",
    )(ssd2, confw2, confr2, x2, w_out_bf, w_out_bf, ln_g, ln_b, g_m, ffn_g, sh_f, sc_f, wr_hi, wr_lo, br_pad)


def _scatter_body(fill0_ref, nfill_ref, dest_ref, h_ref, o_ref, zero_scr, sem):
    tm = h_ref.shape[0]

    def slot_copy(r, k):
        return pltpu.make_async_copy(h_ref.at[pl.ds(r, 1)], o_ref.at[pl.ds(dest_ref[0, 0, r * TOP_K + k], 1)], sem)

    def fill_copy(row):
        return pltpu.make_async_copy(zero_scr.at[pl.ds(0, 1)], o_ref.at[pl.ds(row, 1)], sem)

    def each_fill(fn):
        def per_expert(e, c):
            lax.fori_loop(0, nfill_ref[e], lambda j, cc: fn(fill0_ref[e] + j) or cc, 0)
            return c

        lax.fori_loop(0, N_EXPERTS, per_expert, 0)

    @pl.when(pl.program_id(0) == 0)
    def _():
        zero_scr[...] = jnp.zeros(zero_scr.shape, zero_scr.dtype)
        each_fill(lambda row: fill_copy(row).start())
        each_fill(lambda row: fill_copy(row).wait())

    def issue(r, c):
        for k in range(TOP_K):
            slot_copy(r, k).start(priority=k % 2)
        return c

    lax.fori_loop(0, tm, issue, 0)

    def drain(r, c):
        for k in range(TOP_K):
            slot_copy(r, k).wait()
        return c

    lax.fori_loop(0, tm, drain, 0)


def _dispatch(h2, dest3, fill0, nfill, n_rows):
    m, width = h2.shape
    tm = COMB_TM
    return pl.pallas_call(
        _scatter_body,
        grid_spec=pltpu.PrefetchScalarGridSpec(
            num_scalar_prefetch=2,
            grid=(m // tm,),
            in_specs=[
                pl.BlockSpec((1, 1, tm * TOP_K), lambda i, f0, nf: (i, 0, 0), memory_space=pltpu.SMEM),
                pl.BlockSpec((tm, width), lambda i, f0, nf: (i, 0)),
            ],
            out_specs=pl.BlockSpec(memory_space=pl.ANY),
            scratch_shapes=[pltpu.VMEM((8, width), h2.dtype), pltpu.SemaphoreType.DMA(())],
        ),
        out_shape=jax.ShapeDtypeStruct((n_rows, width), h2.dtype),
        compiler_params=_cparams(("arbitrary",)),
        name="moe_dispatch",
    )(fill0, nfill, dest3, h2)


def _moe_body(be_ref, nsub_ref, bi_ref, x_ref, wg_ref, wl_ref, bg_ref, bl_ref, wd_ref, bd_ref, o_ref,
              wg_scr, wl_scr, wd_scr):
    i = pl.program_id(0)
    f = pl.program_id(1)
    ns = nsub_ref[i]

    @pl.when(ns > 0)
    def _():
        @pl.when(f == 0)
        def _():
            o_ref[...] = jnp.broadcast_to(bd_ref[0], o_ref.shape)

        wg_scr[...] = wg_ref[0].astype(BF16)
        wl_scr[...] = wl_ref[0].astype(BF16)
        wd_scr[...] = wd_ref[0].astype(BF16)

        def sub(s, c):
            r0 = pl.multiple_of(s * MOE_SUB, MOE_SUB)
            xb = x_ref[pl.ds(r0, MOE_SUB), :].astype(BF16)
            gate = _dot(xb, wg_scr[...]) + bg_ref[0]
            lin = _dot(xb, wl_scr[...]) + bl_ref[0]
            glu = jnp.minimum(gate, SWIGLU_LIMIT)
            lin = jnp.clip(lin, -SWIGLU_LIMIT, SWIGLU_LIMIT)
            act = glu * jax.nn.sigmoid(SWIGLU_ALPHA * glu) * (lin + 1.0)
            o_ref[pl.ds(r0, MOE_SUB), :] += _dot(act.astype(BF16), wd_scr[...])
            return c

        lax.fori_loop(0, ns, sub, 0)


def _moe_mlp(xs, block_e, nsub, block_idx, w_gate_up, b_gate_up3, w_down, b_down3, n_super):
    n_ff = D_FF // MOE_TF
    lin0 = D_FF // MOE_TF

    def ff(i, f, ns):
        return jnp.where(ns[i] > 0, f, n_ff - 1)

    return pl.pallas_call(
        _moe_body,
        grid_spec=pltpu.PrefetchScalarGridSpec(
            num_scalar_prefetch=3,
            grid=(n_super, n_ff),
            in_specs=[
                pl.BlockSpec((MOE_SUPER, D_MODEL), lambda i, f, be, ns, bi: (bi[i], 0)),
                pl.BlockSpec((1, D_MODEL, MOE_TF), lambda i, f, be, ns, bi: (be[i], 0, ff(i, f, ns))),
                pl.BlockSpec((1, D_MODEL, MOE_TF), lambda i, f, be, ns, bi: (be[i], 0, lin0 + ff(i, f, ns))),
                pl.BlockSpec((1, 1, MOE_TF), lambda i, f, be, ns, bi: (be[i], 0, ff(i, f, ns))),
                pl.BlockSpec((1, 1, MOE_TF), lambda i, f, be, ns, bi: (be[i], 0, lin0 + ff(i, f, ns))),
                pl.BlockSpec((1, MOE_TF, D_MODEL), lambda i, f, be, ns, bi: (be[i], ff(i, f, ns), 0)),
                pl.BlockSpec((1, 1, D_MODEL), lambda i, f, be, ns, bi: (be[i], 0, 0)),
            ],
            out_specs=pl.BlockSpec((MOE_SUPER, D_MODEL), lambda i, f, be, ns, bi: (bi[i], 0)),
            scratch_shapes=[
                pltpu.VMEM((D_MODEL, MOE_TF), BF16),
                pltpu.VMEM((D_MODEL, MOE_TF), BF16),
                pltpu.VMEM((MOE_TF, D_MODEL), BF16),
            ],
        ),
        out_shape=jax.ShapeDtypeStruct((n_super * MOE_SUPER, D_MODEL), F32),
        compiler_params=_cparams(("arbitrary", "arbitrary")),
        name="moe_mlp",
    )(block_e, nsub, block_idx, xs, w_gate_up, w_gate_up, b_gate_up3, b_gate_up3, w_down, b_down3)


def _combine_body(dest_ref, y_ref, x1_ref, route_ref, gf_ref, fg_ref, o_ref, buf, sem):
    tm = x1_ref.shape[0]

    def issue(r, c):
        for k in range(TOP_K):
            pltpu.make_async_copy(y_ref.at[pl.ds(dest_ref[0, 0, r * TOP_K + k], 1)], buf.at[k, pl.ds(r, 1)],
                                  sem).start(priority=k % 2)
        return c

    lax.fori_loop(0, tm, issue, 0)

    def drain(r, c):
        for k in range(TOP_K):
            pltpu.make_async_copy(y_ref.at[pl.ds(0, 1)], buf.at[k, pl.ds(0, 1)], sem).wait()
        return c

    lax.fori_loop(0, tm, drain, 0)

    route = route_ref[...]
    moe = jnp.zeros(x1_ref.shape, F32)
    for k in range(TOP_K):
        moe = moe + route[:, TOP_K + k:TOP_K + k + 1] * buf[k]
    x2 = x1_ref[...] + gf_ref[0] * moe
    ms = jnp.mean(x2 * x2, axis=-1, keepdims=True)
    o_ref[...] = x2 * lax.rsqrt(ms + EPS) * fg_ref[...]


def _combine(dest3, y, x1, route, g_f, final_g, rows_per_mod):
    m = x1.shape[0]
    tm = COMB_TM
    per = rows_per_mod // tm
    return pl.pallas_call(
        _combine_body,
        grid=(m // tm,),
        in_specs=[
            pl.BlockSpec((1, 1, tm * TOP_K), lambda i: (i, 0, 0), memory_space=pltpu.SMEM),
            pl.BlockSpec(memory_space=pl.ANY),
            pl.BlockSpec((tm, D_MODEL), lambda i: (i, 0)),
            pl.BlockSpec((tm, LANES), lambda i: (i, 0)),
            pl.BlockSpec((1, 1, D_MODEL), lambda i: (i // per, 0, 0)),
            pl.BlockSpec((1, D_MODEL), lambda i: (0, 0)),
        ],
        out_specs=pl.BlockSpec((tm, D_MODEL), lambda i: (i, 0)),
        out_shape=jax.ShapeDtypeStruct((m, D_MODEL), F32),
        scratch_shapes=[pltpu.VMEM((TOP_K, tm, D_MODEL), F32), pltpu.SemaphoreType.DMA(())],
        compiler_params=_cparams(("arbitrary",)),
        name="moe_combine_norm",
    )(dest3, y, x1, route, g_f, final_g)


def _pack_heads(fwd, bwd):
    f = fwd.reshape(SSD_GROUPS, HEADS_PER_GROUP)
    b = bwd.reshape(SSD_GROUPS, HEADS_PER_GROUP)
    z = jnp.zeros((SSD_GROUPS, LANES - 2 * HEADS_PER_GROUP), F32)
    return jnp.concatenate([f, b, z], axis=1).reshape(1, SSD_GROUPS * LANES)


def _routing_tables(route, cnt, n_super):
    idx = route[:, 0:TOP_K].astype(jnp.int32)
    pos = route[:, 2 * TOP_K:3 * TOP_K].astype(jnp.int32)
    counts = cnt[0, :N_EXPERTS].astype(jnp.int32)
    padded = (counts + MOE_SUPER - 1) // MOE_SUPER * MOE_SUPER
    pad_end = jnp.cumsum(padded)
    pad_start = pad_end - padded
    dest = pad_start[idx] + pos
    n_live = pad_end[-1] // MOE_SUPER
    block_idx = jnp.minimum(jnp.arange(n_super, dtype=jnp.int32), n_live - 1)
    blk_start = block_idx * MOE_SUPER
    block_e = jnp.sum((pad_end[None, :] <= blk_start[:, None]).astype(jnp.int32), axis=1)
    block_e = jnp.minimum(block_e, N_EXPERTS - 1)
    live_rows = jnp.clip(counts[block_e] - (blk_start - pad_start[block_e]), 0, MOE_SUPER)
    nsub = (live_rows + MOE_SUB - 1) // MOE_SUB
    nsub = jnp.where(jnp.arange(n_super) < n_live, nsub, 0).astype(jnp.int32)
    fill0 = (pad_start + counts).astype(jnp.int32)
    nfill = ((counts + MOE_SUB - 1) // MOE_SUB * MOE_SUB - counts).astype(jnp.int32)
    return dest, block_e, nsub, block_idx, fill0, nfill


def kernel(x, c, ctx, c_ctx, w_mod, b_mod, norm_mix_g, norm_ffn_g, w_in, ssd_conv_w, ssd_conv_b, ssd_a_log_f, ssd_dt_bias_f, ssd_a_log_b, ssd_dt_bias_b, ssd_d_skip, ssd_norm_g, conf_dw_w, conf_dw_b, conf_ln_g, conf_ln_b, w_out, w_router, b_router, w_gate_up, b_gate_up, w_down, b_down, final_norm_g):
    bsz, n_lat, _ = x.shape
    n_ctx = ctx.shape[1]
    n_tok = bsz * n_lat
    l = 0

    c8 = jnp.zeros((8, D_MODEL), F32).at[:bsz].set(c).at[bsz].set(c_ctx)
    mod = _adaln_mod(c8, w_mod[l], b_mod[l][None, :])
    sh_m, sc_m, g_m, sh_f, sc_f, g_f = [mod[:bsz, k * D_MODEL:(k + 1) * D_MODEL].reshape(bsz, 1, D_MODEL)
                                        for k in range(N_MOD)]
    shc = mod[bsz, 0:D_MODEL].reshape(1, 1, D_MODEL)
    scc = mod[bsz, D_MODEL:2 * D_MODEL].reshape(1, 1, D_MODEL)

    wi = w_in[l]
    w_a = wi[:, :OFF_DT].astype(BF16)
    w_b = wi[:, OFF_CONF:].astype(BF16)
    wf = wi[:, OFF_DT:OFF_DT + SSD_HEADS].reshape(D_MODEL, SSD_GROUPS, HEADS_PER_GROUP)
    wb = wi[:, OFF_DT + SSD_HEADS:OFF_CONF].reshape(D_MODEL, SSD_GROUPS, HEADS_PER_GROUP)
    wz = jnp.zeros((D_MODEL, SSD_GROUPS, LANES - 2 * HEADS_PER_GROUP), F32)
    w_dt = jnp.concatenate([wf, wb, wz], axis=2).reshape(D_MODEL, SSD_GROUPS * LANES).astype(BF16)

    gain_mix = norm_mix_g[l][None, :]
    proj, dt_raw = _in_proj(x.reshape(n_tok, D_MODEL), gain_mix, sh_m, sc_m, w_a, w_b, w_dt,
                            0, PK_N // PROJ_TN, 1024, n_lat)
    n_cb = (D_SSD + 2 * D_BC) // PROJ_TN
    projc, dtc_raw = _in_proj(ctx.reshape(bsz * n_ctx, D_MODEL), gain_mix, shc, scc, w_a, w_b, w_dt,
                              PK_X // PROJ_TN, n_cb, bsz * n_ctx, bsz * n_ctx)
    proj3 = proj.reshape(bsz, n_lat, PK_N)

    ssd_out = _ssd(proj3, dt_raw.reshape(bsz, n_lat, -1), projc.reshape(bsz, n_ctx, -1),
                   dtc_raw.reshape(bsz, n_ctx, -1), ssd_conv_w[l], ssd_conv_b[l][None, :],
                   _pack_heads(ssd_dt_bias_f[l], ssd_dt_bias_b[l]), _pack_heads(ssd_a_log_f[l], ssd_a_log_b[l]),
                   jnp.repeat(ssd_d_skip[l], SSD_HEAD_DIM)[None, :], ssd_norm_g[l][None, :])

    conf_w = _conf_conv(proj3, conf_dw_w[l], conf_dw_b[l][None, :], True)
    conf_r = _conf_conv(proj3, conf_dw_w[l], conf_dw_b[l][None, :], False)

    wr = jnp.zeros((D_MODEL, LANES), F32).at[:, :N_EXPERTS].set(w_router[l])
    wr_hi = wr.astype(BF16)
    wr_lo = (wr - wr_hi.astype(F32)).astype(BF16)
    br = jnp.full((1, LANES), NEG_BIG, F32).at[0, :N_EXPERTS].set(b_router[l])
    half = D_CONF // 2
    x1, h2p, route, cnt = _out_proj(
        ssd_out.reshape(n_tok, D_SSD), conf_w.reshape(n_tok, half), conf_r.reshape(n_tok, half),
        x.reshape(n_tok, D_MODEL), w_out[l].astype(BF16), conf_ln_g[l][None, :], conf_ln_b[l][None, :], g_m,
        norm_ffn_g[l][None, :], sh_f, sc_f, wr_hi, wr_lo, br, n_lat)

    n_super = -(-n_tok * TOP_K // MOE_SUPER) + N_EXPERTS
    dest, block_e, nsub, block_idx, fill0, nfill = _routing_tables(route, cnt, n_super)
    dest3 = dest.reshape(n_tok // COMB_TM, 1, COMB_TM * TOP_K)
    xs = _dispatch(h2p, dest3, fill0, nfill, n_super * MOE_SUPER)
    y = _moe_mlp(xs, block_e, nsub, block_idx, w_gate_up[l], b_gate_up[l][:, None, :], w_down[l], b_down[l][:, None, :], n_super)
    out = _combine(dest3, y, x1, route, g_f, final_norm_g[None, :], n_lat)
    return out.reshape(bsz, n_lat, D_MODEL)
```

```python
import functools

import jax
import jax.numpy as jnp
from jax import lax
from jax.experimental import pallas as pl
from jax.experimental.pallas import tpu as pltpu

F32 = jnp.float32
BF16 = jnp.bfloat16

D_MODEL = 2048
N_MOD = 6
EPS = 1e-6
GRID_W = 64

SSD_HEADS = 32
SSD_HEAD_DIM = 64
D_SSD = SSD_HEADS * SSD_HEAD_DIM
SSD_GROUPS = 4
HEADS_PER_GROUP = SSD_HEADS // SSD_GROUPS
D_GROUP = D_SSD // SSD_GROUPS
D_STATE = 128
CHUNK = 128
D_BC = SSD_GROUPS * D_STATE
D_CONF = 2048
CONF_KERNEL = 31
CONF_HALF = CONF_KERNEL // 2
N_EXPERTS = 32
TOP_K = 4
D_FF = D_MODEL
SWIGLU_LIMIT = 7.0
SWIGLU_ALPHA = 1.702

OFF_XBC = D_SSD
OFF_DT = OFF_XBC + D_SSD + 2 * D_BC
OFF_CONF = OFF_DT + 2 * SSD_HEADS

PK_Z = 0
PK_X = D_SSD
PK_B = PK_X + D_SSD
PK_C = PK_B + D_BC
PK_CA = PK_C + D_BC
PK_CG = PK_CA + D_CONF
PK_N = PK_CG + D_CONF

LANES = 128
VMEM_LIMIT_BYTES = 56 * 1024 * 1024
NEG_BIG = -1e30

PROJ_TN = 512
MOE_SUB = 384
MOE_SUPER = 3 * MOE_SUB
MOE_TF = 256
OUT_TM = 256
COMB_TM = 256
CONF_TC = LANES
CONF_ROWS = 128


def _cparams(sem):
    return pltpu.CompilerParams(dimension_semantics=sem, vmem_limit_bytes=VMEM_LIMIT_BYTES)


def _silu(v):
    return v * jax.nn.sigmoid(v)


def _dot(a, b):
    return jnp.dot(a, b, preferred_element_type=F32)


def _split2(v):
    hi = v.astype(BF16)
    lo = (v - hi.astype(F32)).astype(BF16)
    return hi, lo


def _split3(v):
    hi = v.astype(BF16)
    r1 = v - hi.astype(F32)
    mid = r1.astype(BF16)
    lo = (r1 - mid.astype(F32)).astype(BF16)
    return hi, mid, lo


def _mod_body(c_ref, w_ref, b_ref, o_ref):
    s = _silu(c_ref[...])
    o_ref[...] = _dot(s.astype(BF16), w_ref[...].astype(BF16)) + b_ref[...]


def _adaln_mod(c8, w_mod, b_mod):
    tn = 1024
    n = w_mod.shape[1]
    return pl.pallas_call(
        _mod_body,
        grid=(n // tn,),
        in_specs=[
            pl.BlockSpec((8, D_MODEL), lambda j: (0, 0)),
            pl.BlockSpec((D_MODEL, tn), lambda j: (0, j)),
            pl.BlockSpec((1, tn), lambda j: (0, j)),
        ],
        out_specs=pl.BlockSpec((8, tn), lambda j: (0, j)),
        out_shape=jax.ShapeDtypeStruct((8, n), F32),
        compiler_params=_cparams(("arbitrary",)),
        name="adaln_mod",
    )(c8, w_mod, b_mod)


def _inproj_body(x_ref, g_ref, sh_ref, sc_ref, wa_ref, wb_ref, wdt_ref, o_ref, odt_ref, h_scr, *, col_block0, n_a):
    @pl.when(pl.program_id(1) == 0)
    def _():
        xf = x_ref[...]
        ms = jnp.mean(xf * xf, axis=-1, keepdims=True)
        y = xf * lax.rsqrt(ms + EPS) * g_ref[...]
        hb = (y * (1.0 + sc_ref[0]) + sh_ref[0]).astype(BF16)
        h_scr[...] = hb
        odt_ref[...] = _dot(hb, wdt_ref[...])

    j = pl.program_id(1) + col_block0

    @pl.when(j < n_a)
    def _():
        o_ref[...] = _dot(h_scr[...], wa_ref[...]).astype(o_ref.dtype)

    @pl.when(j >= n_a)
    def _():
        o_ref[...] = _dot(h_scr[...], wb_ref[...]).astype(o_ref.dtype)


def _in_proj(x2d, gain, shift, scale, w_a, w_b, w_dt, col_block0, n_col_blocks, tm, rows_per_mod):
    m = x2d.shape[0]
    mods_per_tile = rows_per_mod // tm
    n_dt = w_dt.shape[1]
    n_a = OFF_DT // PROJ_TN
    return pl.pallas_call(
        functools.partial(_inproj_body, col_block0=col_block0, n_a=n_a),
        grid=(m // tm, n_col_blocks),
        in_specs=[
            pl.BlockSpec((tm, D_MODEL), lambda i, j: (i, 0)),
            pl.BlockSpec((1, D_MODEL), lambda i, j: (0, 0)),
            pl.BlockSpec((1, 1, D_MODEL), lambda i, j: (i // mods_per_tile, 0, 0)),
            pl.BlockSpec((1, 1, D_MODEL), lambda i, j: (i // mods_per_tile, 0, 0)),
            pl.BlockSpec((D_MODEL, PROJ_TN), lambda i, j: (0, jnp.minimum(col_block0 + j, n_a - 1))),
            pl.BlockSpec((D_MODEL, PROJ_TN), lambda i, j: (0, jnp.maximum(col_block0 + j - n_a, 0))),
            pl.BlockSpec((D_MODEL, n_dt), lambda i, j: (0, 0)),
        ],
        out_specs=[
            pl.BlockSpec((tm, PROJ_TN), lambda i, j: (i, j)),
            pl.BlockSpec((tm, n_dt), lambda i, j: (i, 0)),
        ],
        out_shape=[
            jax.ShapeDtypeStruct((m, n_col_blocks * PROJ_TN), BF16),
            jax.ShapeDtypeStruct((m, n_dt), F32),
        ],
        scratch_shapes=[pltpu.VMEM((tm, D_MODEL), BF16)],
        compiler_params=_cparams(("arbitrary", "arbitrary")),
        name="in_proj",
    )(x2d, gain, shift, scale, w_a, w_b, w_dt)


CONV_ROWS = 256
PAD = 8


def _conv3_silu(src_ref, pad_ref, w_ref, b_ref, dst_ref, n_rows):
    ncol = pad_ref.shape[1]
    zero = jnp.zeros((PAD, ncol), F32)
    pad_ref[0:PAD, :] = zero
    pad_ref[PAD + n_rows:PAD + n_rows + PAD, :] = zero
    win = CONV_ROWS + 2 * PAD

    def fill(i, c):
        s = pl.multiple_of(i * CONV_ROWS, CONV_ROWS)
        pad_ref[pl.ds(s + PAD, CONV_ROWS), :] = src_ref[0, pl.ds(s, CONV_ROWS), :].astype(F32)
        return c

    lax.fori_loop(0, n_rows // CONV_ROWS, fill, 0)

    def step(i, c):
        s = pl.multiple_of(i * CONV_ROWS, CONV_ROWS)
        w = pad_ref[pl.ds(s, win), :]
        prev = pltpu.roll(w, 1, 0)[PAD:PAD + CONV_ROWS]
        nxt = pltpu.roll(w, win - 1, 0)[PAD:PAD + CONV_ROWS]
        cur = w[PAD:PAD + CONV_ROWS]
        y = prev * w_ref[0:1, :] + cur * w_ref[1:2, :] + nxt * w_ref[2:3, :] + b_ref[...]
        dst_ref[pl.ds(s, CONV_ROWS), :] = _silu(y)
        return c

    lax.fori_loop(0, n_rows // CONV_ROWS, step, 0)


def _tri_dot(tri, v):
    hi, mid, lo = _split3(v)
    return _dot(tri, hi) + _dot(tri, mid) + _dot(tri, lo)


def _expand(v, e_mat):
    hi, lo = _split2(v)
    return _dot(hi, e_mat) + _dot(lo, e_mat)


def _ssd_dir_step(xs, bmat, cmat, cums, dts, st_ref, e_mat, is_fwd):
    edge = cums[CHUNK - 1:CHUNK, :] if is_fwd else cums[0:1, :]
    wdec = jnp.exp(edge - cums) * dts
    xw = (xs * _expand(wdec, e_mat)).astype(BF16)
    s_loc = _dot(bmat.T.astype(BF16), xw)
    tot = _expand(jnp.broadcast_to(jnp.exp(edge), (8, LANES)), e_mat)[0:1, :]
    s_prev = st_ref[...]
    y_off = None
    if cmat is not None:
        y_off = _dot(cmat.astype(BF16), s_prev.astype(BF16)) * _expand(jnp.exp(cums), e_mat)
    st_ref[...] = s_prev * tot + s_loc
    return y_off


def _ssd_diag(xs, cb, cum, rc, dt):
    cum_t, rc_t, dt_t = cum.T, rc.T, dt.T
    ii = lax.broadcasted_iota(jnp.int32, (CHUNK, CHUNK), 0)
    jj = lax.broadcasted_iota(jnp.int32, (CHUNK, CHUNK), 1)
    lower, upper = ii >= jj, jj >= ii
    first_half = jj < SSD_HEAD_DIM
    outs = []
    for pair in range(HEADS_PER_GROUP // 2):
        ms = []
        for h in (2 * pair, 2 * pair + 1):
            hb = HEADS_PER_GROUP + h
            lf = jnp.exp(jnp.where(lower, cum[:, h:h + 1] - cum_t[h:h + 1, :], NEG_BIG)) * dt_t[h:h + 1, :]
            lb = jnp.exp(jnp.where(upper, rc[:, hb:hb + 1] - rc_t[hb:hb + 1, :], NEG_BIG)) * dt_t[hb:hb + 1, :]
            ms.append((cb * (lf + lb)).astype(BF16))
        xp = xs[:, pair * LANES:(pair + 1) * LANES]
        rhs = jnp.concatenate([jnp.where(first_half, xp, 0.0), jnp.where(first_half, 0.0, xp)], axis=0)
        outs.append(_dot(jnp.concatenate(ms, axis=1), rhs.astype(BF16)))
    return jnp.concatenate(outs, axis=1)


def _ssd_body(xl_ref, bl_ref, cl_ref, zl_ref, dtl_ref, xc_ref, bc_ref, dtc_ref,
              cwx_ref, cwb_ref, cwc_ref, cbx_ref, cbb_ref, cbc_ref,
              dtbias_ref, alog_ref, dskip_ref, ng_ref,
              o_ref,
              xpad, bpad, xs_scr, bs_scr, cs_scr, a_scr, dt_scr,
              xcpad, bcpad, xsc_scr, bsc_scr, ac_scr, dtcs_scr, y_scr, yb_scr, stf_scr, stb_scr):
    t_lat = xs_scr.shape[0]
    t_ctx = xsc_scr.shape[0]
    n_lat = t_lat // CHUNK
    n_ctx = t_ctx // CHUNK

    _conv3_silu(xl_ref, xpad, cwx_ref, cbx_ref, xs_scr, t_lat)
    _conv3_silu(bl_ref, bpad, cwb_ref, cbb_ref, bs_scr, t_lat)
    _conv3_silu(cl_ref, bpad, cwc_ref, cbc_ref, cs_scr, t_lat)
    _conv3_silu(xc_ref, xcpad, cwx_ref, cbx_ref, xsc_scr, t_ctx)
    _conv3_silu(bc_ref, bcpad, cwb_ref, cbb_ref, bsc_scr, t_ctx)
    neg_a = -jnp.exp(alog_ref[...])
    dt = jax.nn.softplus(dtl_ref[0] + dtbias_ref[...])
    dt_scr[...] = dt
    a_scr[...] = dt * neg_a
    dtc = jax.nn.softplus(dtc_ref[0] + dtbias_ref[...])
    dtcs_scr[...] = dtc
    ac_scr[...] = dtc * neg_a

    ii = lax.broadcasted_iota(jnp.int32, (CHUNK, CHUNK), 0)
    jj = lax.broadcasted_iota(jnp.int32, (CHUNK, CHUNK), 1)
    tril = (ii >= jj).astype(BF16)
    triu = (jj >= ii).astype(BF16)
    er = lax.broadcasted_iota(jnp.int32, (LANES, D_GROUP), 0)
    eh = lax.broadcasted_iota(jnp.int32, (LANES, D_GROUP), 1) // SSD_HEAD_DIM
    e_f = (er == eh).astype(BF16)
    e_b = (er == eh + HEADS_PER_GROUP).astype(BF16)

    def ctx_step(c, is_fwd):
        s = pl.multiple_of(c * CHUNK, CHUNK)
        cums = _tri_dot(tril if is_fwd else triu, ac_scr[pl.ds(s, CHUNK), :])
        _ssd_dir_step(xsc_scr[pl.ds(s, CHUNK), :], bsc_scr[pl.ds(s, CHUNK), :], None, cums,
                      dtcs_scr[pl.ds(s, CHUNK), :], stf_scr if is_fwd else stb_scr, e_f if is_fwd else e_b, is_fwd)

    def lat_step(c, carry):
        s = pl.multiple_of(c * CHUNK, CHUNK)
        a_blk = a_scr[pl.ds(s, CHUNK), :]
        dt_blk = dt_scr[pl.ds(s, CHUNK), :]
        cum = _tri_dot(tril, a_blk)
        rc = _tri_dot(triu, a_blk)
        xs = xs_scr[pl.ds(s, CHUNK), :]
        bm = bs_scr[pl.ds(s, CHUNK), :]
        cm = cs_scr[pl.ds(s, CHUNK), :]
        cb = lax.dot_general(cm.astype(BF16), bm.astype(BF16), (((1,), (1,)), ((), ())),
                             preferred_element_type=F32)
        y_diag = _ssd_diag(xs, cb, cum, rc, dt_blk)
        y_off = _ssd_dir_step(xs, bm, cm, cum, dt_blk, stf_scr, e_f, True)
        y_scr[pl.ds(s, CHUNK), :] = y_diag + y_off

        sb = pl.multiple_of((n_lat - 1 - c) * CHUNK, CHUNK)
        rcb = _tri_dot(triu, a_scr[pl.ds(sb, CHUNK), :])
        yb_scr[pl.ds(sb, CHUNK), :] = _ssd_dir_step(
            xs_scr[pl.ds(sb, CHUNK), :], bs_scr[pl.ds(sb, CHUNK), :], cs_scr[pl.ds(sb, CHUNK), :], rcb,
            dt_scr[pl.ds(sb, CHUNK), :], stb_scr, e_b, False)
        return carry

    stf_scr[...] = jnp.zeros(stf_scr.shape, F32)
    stb_scr[...] = jnp.zeros(stb_scr.shape, F32)
    for c in range(n_ctx):
        ctx_step(c, True)
    for c in reversed(range(n_ctx)):
        ctx_step(c, False)
    lax.fori_loop(0, n_lat, lat_step, 0, unroll=4)

    def finish(i, carry):
        s = pl.multiple_of(i * CONV_ROWS, CONV_ROWS)
        y = y_scr[pl.ds(s, CONV_ROWS), :] + yb_scr[pl.ds(s, CONV_ROWS), :]
        y = y + dskip_ref[...] * xs_scr[pl.ds(s, CONV_ROWS), :]
        y = y * _silu(zl_ref[0, pl.ds(s, CONV_ROWS), :].astype(F32))
        ms = jnp.mean(y * y, axis=-1, keepdims=True)
        o_ref[0, pl.ds(s, CONV_ROWS), :] = (y * lax.rsqrt(ms + EPS) * ng_ref[...]).astype(o_ref.dtype)
        return carry

    lax.fori_loop(0, t_lat // CONV_ROWS, finish, 0)


def _ssd(proj3, dt3, projc3, dtc3, conv_w, conv_b, dtbias_pk, alog_pk, dskip_row, norm_g_row):
    bsz, t_lat, _ = proj3.shape
    t_ctx = projc3.shape[1]
    xb, bb, cb_ = PK_X // D_GROUP, PK_B // D_STATE, PK_C // D_STATE
    cxb, cbb = 0, D_SSD // D_STATE
    wx0, wb0, wc0 = 0, D_SSD // D_STATE, (D_SSD + D_BC) // D_STATE
    in_specs = [
        pl.BlockSpec((1, t_lat, D_GROUP), lambda b, g: (b, 0, xb + g)),
        pl.BlockSpec((1, t_lat, D_STATE), lambda b, g: (b, 0, bb + g)),
        pl.BlockSpec((1, t_lat, D_STATE), lambda b, g: (b, 0, cb_ + g)),
        pl.BlockSpec((1, t_lat, D_GROUP), lambda b, g: (b, 0, g)),
        pl.BlockSpec((1, t_lat, LANES), lambda b, g: (b, 0, g)),
        pl.BlockSpec((1, t_ctx, D_GROUP), lambda b, g: (b, 0, cxb + g)),
        pl.BlockSpec((1, t_ctx, D_STATE), lambda b, g: (b, 0, cbb + g)),
        pl.BlockSpec((1, t_ctx, LANES), lambda b, g: (b, 0, g)),
        pl.BlockSpec((3, D_GROUP), lambda b, g: (0, wx0 + g)),
        pl.BlockSpec((3, D_STATE), lambda b, g: (0, wb0 + g)),
        pl.BlockSpec((3, D_STATE), lambda b, g: (0, wc0 + g)),
        pl.BlockSpec((1, D_GROUP), lambda b, g: (0, wx0 + g)),
        pl.BlockSpec((1, D_STATE), lambda b, g: (0, wb0 + g)),
        pl.BlockSpec((1, D_STATE), lambda b, g: (0, wc0 + g)),
        pl.BlockSpec((1, LANES), lambda b, g: (0, g)),
        pl.BlockSpec((1, LANES), lambda b, g: (0, g)),
        pl.BlockSpec((1, D_GROUP), lambda b, g: (0, g)),
        pl.BlockSpec((1, D_GROUP), lambda b, g: (0, g)),
    ]
    scratch = [
        pltpu.VMEM((t_lat + 2 * PAD, D_GROUP), F32),
        pltpu.VMEM((t_lat + 2 * PAD, D_STATE), F32),
        pltpu.VMEM((t_lat, D_GROUP), F32),
        pltpu.VMEM((t_lat, D_STATE), F32),
        pltpu.VMEM((t_lat, D_STATE), F32),
        pltpu.VMEM((t_lat, LANES), F32),
        pltpu.VMEM((t_lat, LANES), F32),
        pltpu.VMEM((t_ctx + 2 * PAD, D_GROUP), F32),
        pltpu.VMEM((t_ctx + 2 * PAD, D_STATE), F32),
        pltpu.VMEM((t_ctx, D_GROUP), F32),
        pltpu.VMEM((t_ctx, D_STATE), F32),
        pltpu.VMEM((t_ctx, LANES), F32),
        pltpu.VMEM((t_ctx, LANES), F32),
        pltpu.VMEM((t_lat, D_GROUP), F32),
        pltpu.VMEM((t_lat, D_GROUP), F32),
        pltpu.VMEM((D_STATE, D_GROUP), F32),
        pltpu.VMEM((D_STATE, D_GROUP), F32),
    ]
    return pl.pallas_call(
        _ssd_body,
        grid=(bsz, SSD_GROUPS),
        in_specs=in_specs,
        out_specs=pl.BlockSpec((1, t_lat, D_GROUP), lambda b, g: (b, 0, g)),
        out_shape=jax.ShapeDtypeStruct((bsz, t_lat, D_SSD), BF16),
        scratch_shapes=scratch,
        compiler_params=_cparams(("arbitrary", "arbitrary")),
        name="ssd_bidir",
    )(proj3, proj3, proj3, proj3, dt3, projc3, projc3, dtc3,
      conv_w, conv_w, conv_w, conv_b, conv_b, conv_b, dtbias_pk, alog_pk, dskip_row, norm_g_row)


def _conf_body(a_ref, g_ref, w_ref, b_ref, o_ref, v_scr, p_scr, c_scr, *, along_w):
    t = a_ref.shape[1]
    rows = t // GRID_W
    minor = rows if along_w else GRID_W
    halo = CONF_HALF * minor
    tc = a_ref.shape[2]
    v_pitch = GRID_W + 8
    c_pitch = rows + 8

    if along_w:
        def glu(r, c):
            s = pl.multiple_of(r * GRID_W, GRID_W)
            a = a_ref[0, pl.ds(s, GRID_W), :].astype(F32)
            g = g_ref[0, pl.ds(s, GRID_W), :].astype(F32)
            v_scr[pl.ds(pl.multiple_of(r * v_pitch, 8), GRID_W), :] = a * jax.nn.sigmoid(g)
            return c

        lax.fori_loop(0, rows, glu, 0)
    else:
        def glu(i, c):
            s = pl.multiple_of(i * CONV_ROWS, CONV_ROWS)
            a = a_ref[0, pl.ds(s, CONV_ROWS), :].astype(F32)
            g = g_ref[0, pl.ds(s, CONV_ROWS), :].astype(F32)
            p_scr[pl.ds(s + halo, CONV_ROWS), :] = a * jax.nn.sigmoid(g)
            return c

        lax.fori_loop(0, t // CONV_ROWS, glu, 0)
    p_scr[0:halo, :] = jnp.zeros((halo, tc), F32)
    p_scr[halo + t:halo + t + halo, :] = jnp.zeros((halo, tc), F32)
    if along_w:
        for w in range(GRID_W):
            p_scr[halo + w * rows:halo + (w + 1) * rows, :] = v_scr[pl.ds(w, rows, stride=v_pitch), :]

    def conv(i, c):
        s = pl.multiple_of(i * CONF_ROWS, CONF_ROWS)
        acc = jnp.broadcast_to(b_ref[...], (CONF_ROWS, tc))
        for k in range(CONF_KERNEL):
            acc = acc + w_ref[k:k + 1, :] * p_scr[pl.ds(s + k * minor, CONF_ROWS), :]
        if along_w:
            per = CONF_ROWS // rows
            for q in range(per):
                c_scr[pl.ds(pl.multiple_of((i * per + q) * c_pitch, 8), rows), :] = acc[q * rows:(q + 1) * rows]
        else:
            o_ref[0, pl.ds(s, CONF_ROWS), :] = acc.astype(o_ref.dtype)
        return c

    lax.fori_loop(0, t // CONF_ROWS, conv, 0)
    if along_w:
        for r in range(rows):
            o_ref[0, r * GRID_W:(r + 1) * GRID_W, :] = c_scr[pl.ds(r, GRID_W, stride=c_pitch), :].astype(o_ref.dtype)


def _conf_conv(proj3, dw_w, dw_b_row, along_w):
    bsz, t, _ = proj3.shape
    rows = t // GRID_W
    half = D_CONF // 2
    c0 = 0 if along_w else half
    a0, g0, w0 = (PK_CA + c0) // CONF_TC, (PK_CG + c0) // CONF_TC, c0 // CONF_TC
    halo = CONF_HALF * (rows if along_w else GRID_W)
    return pl.pallas_call(
        functools.partial(_conf_body, along_w=along_w),
        grid=(bsz, half // CONF_TC),
        in_specs=[
            pl.BlockSpec((1, t, CONF_TC), lambda b, j: (b, 0, a0 + j)),
            pl.BlockSpec((1, t, CONF_TC), lambda b, j: (b, 0, g0 + j)),
            pl.BlockSpec((CONF_KERNEL, CONF_TC), lambda b, j: (0, w0 + j)),
            pl.BlockSpec((1, CONF_TC), lambda b, j: (0, w0 + j)),
        ],
        out_specs=pl.BlockSpec((1, t, CONF_TC), lambda b, j: (b, 0, j)),
        out_shape=jax.ShapeDtypeStruct((bsz, t, half), BF16),
        scratch_shapes=[
            pltpu.VMEM((rows * (GRID_W + 8), CONF_TC), F32),
            pltpu.VMEM((t + 2 * halo, CONF_TC), F32),
            pltpu.VMEM((GRID_W * (rows + 8), CONF_TC), F32),
        ],
        compiler_params=_cparams(("arbitrary", "arbitrary")),
        name="conf_conv_w" if along_w else "conf_conv_r",
    )(proj3, proj3, dw_w, dw_b_row)


def _outproj_body(ssd_ref, cw_ref, cr_ref, x_ref, wt_ref, wb_ref, lng_ref, lnb_ref, gm_ref,
                  ng_ref, shf_ref, scf_ref, wrh_ref, wrl_ref, br_ref,
                  x1_ref, h2_ref, route_ref, cnt_ref, cnt_scr):
    i = pl.program_id(0)
    tm = x_ref.shape[0]

    @pl.when(i == 0)
    def _():
        cnt_scr[...] = jnp.zeros(cnt_scr.shape, F32)

    v = jnp.concatenate([cw_ref[...].astype(F32), cr_ref[...].astype(F32)], axis=1)
    mu = jnp.mean(v, axis=-1, keepdims=True)
    vc = v - mu
    var = jnp.mean(vc * vc, axis=-1, keepdims=True)
    cf = _silu(vc * lax.rsqrt(var + EPS) * lng_ref[...] + lnb_ref[...])
    acc = _dot(ssd_ref[...], wt_ref[...]) + _dot(cf.astype(BF16), wb_ref[...])
    x1 = x_ref[...] + gm_ref[0] * acc
    x1_ref[...] = x1

    ms = jnp.mean(x1 * x1, axis=-1, keepdims=True)
    h2 = x1 * lax.rsqrt(ms + EPS) * ng_ref[...] * (1.0 + scf_ref[0]) + shf_ref[0]
    h2_ref[...] = h2

    hh, hl = _split2(h2)
    logit = _dot(hh, wrh_ref[...]) + _dot(hl, wrh_ref[...]) + _dot(hh, wrl_ref[...]) + br_ref[...]
    lane = lax.broadcasted_iota(jnp.int32, (tm, LANES), 1).astype(F32)
    tops, hots, idxs = [], [], []
    for _ in range(TOP_K):
        m = jnp.max(logit, axis=-1, keepdims=True)
        idx = jnp.min(jnp.where(logit == m, lane, float(LANES)), axis=-1, keepdims=True)
        hot = lane == idx
        tops.append(m)
        hots.append(hot)
        idxs.append(idx)
        logit = jnp.where(hot, -jnp.inf, logit)
    es = [jnp.exp(m - tops[0]) for m in tops]
    den = es[0] + es[1] + es[2] + es[3]

    sel = jnp.zeros((tm, LANES), F32)
    for hot in hots:
        sel = sel + hot.astype(F32)
    ii = lax.broadcasted_iota(jnp.int32, (tm, tm), 0)
    jj = lax.broadcasted_iota(jnp.int32, (tm, tm), 1)
    before = _dot((ii > jj).astype(BF16), sel.astype(BF16)) + cnt_scr[0:1, :]
    packed = jnp.zeros((tm, LANES), F32)
    for k in range(TOP_K):
        pos = jnp.sum(jnp.where(hots[k], before, 0.0), axis=-1, keepdims=True)
        packed = packed + jnp.where(lane == float(k), idxs[k], 0.0)
        packed = packed + jnp.where(lane == float(TOP_K + k), es[k] / den, 0.0)
        packed = packed + jnp.where(lane == float(2 * TOP_K + k), pos, 0.0)
    route_ref[...] = packed
    cnt_scr[...] = cnt_scr[...] + jnp.sum(sel, axis=0, keepdims=True)
    cnt_ref[...] = cnt_scr[...]


def _out_proj(ssd2, confw2, confr2, x2, w_out_bf, ln_g, ln_b, g_m, ffn_g, sh_f, sc_f, wr_hi, wr_lo, br_pad,
              rows_per_mod):
    m = x2.shape[0]
    tm = OUT_TM
    per = rows_per_mod // tm
    half = D_CONF // 2
    row = lambda i: (i, 0)
    fixed = lambda i: (0, 0)
    mod = lambda i: (i // per, 0, 0)
    return pl.pallas_call(
        _outproj_body,
        grid=(m // tm,),
        in_specs=[
            pl.BlockSpec((tm, D_SSD), row),
            pl.BlockSpec((tm, half), row),
            pl.BlockSpec((tm, half), row),
            pl.BlockSpec((tm, D_MODEL), row),
            pl.BlockSpec((D_SSD, D_MODEL), lambda i: (0, 0), pipeline_mode=pl.Buffered(1)),
            pl.BlockSpec((D_CONF, D_MODEL), lambda i: (1, 0), pipeline_mode=pl.Buffered(1)),
            pl.BlockSpec((1, D_CONF), fixed),
            pl.BlockSpec((1, D_CONF), fixed),
            pl.BlockSpec((1, 1, D_MODEL), mod),
            pl.BlockSpec((1, D_MODEL), fixed),
            pl.BlockSpec((1, 1, D_MODEL), mod),
            pl.BlockSpec((1, 1, D_MODEL), mod),
            pl.BlockSpec((D_MODEL, LANES), fixed),
            pl.BlockSpec((D_MODEL, LANES), fixed),
            pl.BlockSpec((1, LANES), fixed),
        ],
        out_specs=[
            pl.BlockSpec((tm, D_MODEL), row),
            pl.BlockSpec((tm, D_MODEL), row),
            pl.BlockSpec((tm, LANES), row),
            pl.BlockSpec((8, LANES), fixed),
        ],
        out_shape=[
            jax.ShapeDtypeStruct((m, D_MODEL), F32),
            jax.ShapeDtypeStruct((m, D_MODEL), F32),
            jax.ShapeDtypeStruct((m, LANES), F32),
            jax.ShapeDtypeStruct((8, LANES), F32),
        ],
        scratch_shapes=[pltpu.VMEM((8, LANES), F32)],
        compiler_params=_cparams(("arbitrary",)),
        name="out_proj_route",
    )(ssd2, confw2, confr2, x2, w_out_bf, w_out_bf, ln_g, ln_b, g_m, ffn_g, sh_f, sc_f, wr_hi, wr_lo, br_pad)


def _scatter_body(fill0_ref, nfill_ref, dest_ref, h_ref, o_ref, zero_scr, sem):
    tm = h_ref.shape[0]

    def slot_copy(r, k):
        return pltpu.make_async_copy(h_ref.at[pl.ds(r, 1)], o_ref.at[pl.ds(dest_ref[0, 0, r * TOP_K + k], 1)], sem)

    def fill_copy(row):
        return pltpu.make_async_copy(zero_scr.at[pl.ds(0, 1)], o_ref.at[pl.ds(row, 1)], sem)

    def each_fill(fn):
        def per_expert(e, c):
            lax.fori_loop(0, nfill_ref[e], lambda j, cc: fn(fill0_ref[e] + j) or cc, 0)
            return c

        lax.fori_loop(0, N_EXPERTS, per_expert, 0)

    @pl.when(pl.program_id(0) == 0)
    def _():
        zero_scr[...] = jnp.zeros(zero_scr.shape, zero_scr.dtype)
        each_fill(lambda row: fill_copy(row).start())
        each_fill(lambda row: fill_copy(row).wait())

    def issue(r, c):
        for k in range(TOP_K):
            slot_copy(r, k).start()
        return c

    lax.fori_loop(0, tm, issue, 0)

    def drain(r, c):
        for k in range(TOP_K):
            slot_copy(r, k).wait()
        return c

    lax.fori_loop(0, tm, drain, 0, unroll=4)


def _dispatch(h2, dest3, fill0, nfill, n_rows):
    m, width = h2.shape
    tm = COMB_TM
    return pl.pallas_call(
        _scatter_body,
        grid_spec=pltpu.PrefetchScalarGridSpec(
            num_scalar_prefetch=2,
            grid=(m // tm,),
            in_specs=[
                pl.BlockSpec((1, 1, tm * TOP_K), lambda i, f0, nf: (i, 0, 0), memory_space=pltpu.SMEM),
                pl.BlockSpec((tm, width), lambda i, f0, nf: (i, 0)),
            ],
            out_specs=pl.BlockSpec(memory_space=pl.ANY),
            scratch_shapes=[pltpu.VMEM((8, width), h2.dtype), pltpu.SemaphoreType.DMA(())],
        ),
        out_shape=jax.ShapeDtypeStruct((n_rows, width), h2.dtype),
        compiler_params=_cparams(("arbitrary",)),
        name="moe_dispatch",
    )(fill0, nfill, dest3, h2)


def _moe_body(be_ref, nsub_ref, bi_ref, x_ref, wg_ref, wl_ref, bg_ref, bl_ref, wd_ref, bd_ref, o_ref,
              wg_scr, wl_scr, wd_scr):
    i = pl.program_id(0)
    f = pl.program_id(1)
    ns = nsub_ref[i]

    @pl.when(ns > 0)
    def _():
        @pl.when(f == 0)
        def _():
            o_ref[...] = jnp.broadcast_to(bd_ref[0], o_ref.shape)

        wg_scr[...] = wg_ref[0].astype(BF16)
        wl_scr[...] = wl_ref[0].astype(BF16)
        wd_scr[...] = wd_ref[0].astype(BF16)

        def sub(s, c):
            r0 = pl.multiple_of(s * MOE_SUB, MOE_SUB)
            xb = x_ref[pl.ds(r0, MOE_SUB), :].astype(BF16)
            gate = _dot(xb, wg_scr[...]) + bg_ref[0]
            lin = _dot(xb, wl_scr[...]) + bl_ref[0]
            glu = jnp.minimum(gate, SWIGLU_LIMIT)
            lin = jnp.clip(lin, -SWIGLU_LIMIT, SWIGLU_LIMIT)
            act = glu * jax.nn.sigmoid(SWIGLU_ALPHA * glu) * (lin + 1.0)
            o_ref[pl.ds(r0, MOE_SUB), :] += _dot(act.astype(BF16), wd_scr[...])
            return c

        lax.fori_loop(0, ns, sub, 0)


def _moe_mlp(xs, block_e, nsub, block_idx, w_gate_up, b_gate_up3, w_down, b_down3, n_super):
    n_ff = D_FF // MOE_TF
    lin0 = D_FF // MOE_TF

    def ff(i, f, ns):
        return jnp.where(ns[i] > 0, f, n_ff - 1)

    return pl.pallas_call(
        _moe_body,
        grid_spec=pltpu.PrefetchScalarGridSpec(
            num_scalar_prefetch=3,
            grid=(n_super, n_ff),
            in_specs=[
                pl.BlockSpec((MOE_SUPER, D_MODEL), lambda i, f, be, ns, bi: (bi[i], 0)),
                pl.BlockSpec((1, D_MODEL, MOE_TF), lambda i, f, be, ns, bi: (be[i], 0, ff(i, f, ns))),
                pl.BlockSpec((1, D_MODEL, MOE_TF), lambda i, f, be, ns, bi: (be[i], 0, lin0 + ff(i, f, ns))),
                pl.BlockSpec((1, 1, MOE_TF), lambda i, f, be, ns, bi: (be[i], 0, ff(i, f, ns))),
                pl.BlockSpec((1, 1, MOE_TF), lambda i, f, be, ns, bi: (be[i], 0, lin0 + ff(i, f, ns))),
                pl.BlockSpec((1, MOE_TF, D_MODEL), lambda i, f, be, ns, bi: (be[i], ff(i, f, ns), 0)),
                pl.BlockSpec((1, 1, D_MODEL), lambda i, f, be, ns, bi: (be[i], 0, 0)),
            ],
            out_specs=pl.BlockSpec((MOE_SUPER, D_MODEL), lambda i, f, be, ns, bi: (bi[i], 0)),
            scratch_shapes=[
                pltpu.VMEM((D_MODEL, MOE_TF), BF16),
                pltpu.VMEM((D_MODEL, MOE_TF), BF16),
                pltpu.VMEM((MOE_TF, D_MODEL), BF16),
            ],
        ),
        out_shape=jax.ShapeDtypeStruct((n_super * MOE_SUPER, D_MODEL), F32),
        compiler_params=_cparams(("arbitrary", "arbitrary")),
        name="moe_mlp",
    )(block_e, nsub, block_idx, xs, w_gate_up, w_gate_up, b_gate_up3, b_gate_up3, w_down, b_down3)


def _combine_body(dest_ref, y_ref, x1_ref, route_ref, gf_ref, fg_ref, o_ref, buf, sem):
    tm = x1_ref.shape[0]

    def issue(r, c):
        for k in range(TOP_K):
            pltpu.make_async_copy(y_ref.at[pl.ds(dest_ref[0, 0, r * TOP_K + k], 1)], buf.at[k, pl.ds(r, 1)], sem).start()
        return c

    lax.fori_loop(0, tm, issue, 0)

    def drain(r, c):
        for k in range(TOP_K):
            pltpu.make_async_copy(y_ref.at[pl.ds(0, 1)], buf.at[k, pl.ds(0, 1)], sem).wait()
        return c

    lax.fori_loop(0, tm, drain, 0, unroll=4)

    route = route_ref[...]
    moe = jnp.zeros(x1_ref.shape, F32)
    for k in range(TOP_K):
        moe = moe + route[:, TOP_K + k:TOP_K + k + 1] * buf[k]
    x2 = x1_ref[...] + gf_ref[0] * moe
    ms = jnp.mean(x2 * x2, axis=-1, keepdims=True)
    o_ref[...] = x2 * lax.rsqrt(ms + EPS) * fg_ref[...]


def _combine(dest3, y, x1, route, g_f, final_g, rows_per_mod):
    m = x1.shape[0]
    tm = COMB_TM
    per = rows_per_mod // tm
    return pl.pallas_call(
        _combine_body,
        grid=(m // tm,),
        in_specs=[
            pl.BlockSpec((1, 1, tm * TOP_K), lambda i: (i, 0, 0), memory_space=pltpu.SMEM),
            pl.BlockSpec(memory_space=pl.ANY),
            pl.BlockSpec((tm, D_MODEL), lambda i: (i, 0)),
            pl.BlockSpec((tm, LANES), lambda i: (i, 0)),
            pl.BlockSpec((1, 1, D_MODEL), lambda i: (i // per, 0, 0)),
            pl.BlockSpec((1, D_MODEL), lambda i: (0, 0)),
        ],
        out_specs=pl.BlockSpec((tm, D_MODEL), lambda i: (i, 0)),
        out_shape=jax.ShapeDtypeStruct((m, D_MODEL), F32),
        scratch_shapes=[pltpu.VMEM((TOP_K, tm, D_MODEL), F32), pltpu.SemaphoreType.DMA(())],
        compiler_params=_cparams(("arbitrary",)),
        name="moe_combine_norm",
    )(dest3, y, x1, route, g_f, final_g)


def _pack_heads(fwd, bwd):
    f = fwd.reshape(SSD_GROUPS, HEADS_PER_GROUP)
    b = bwd.reshape(SSD_GROUPS, HEADS_PER_GROUP)
    z = jnp.zeros((SSD_GROUPS, LANES - 2 * HEADS_PER_GROUP), F32)
    return jnp.concatenate([f, b, z], axis=1).reshape(1, SSD_GROUPS * LANES)


def _routing_tables(route, cnt, n_super):
    idx = route[:, 0:TOP_K].astype(jnp.int32).reshape(-1)
    pos = route[:, 2 * TOP_K:3 * TOP_K].astype(jnp.int32).reshape(-1)
    counts = cnt[0, :N_EXPERTS].astype(jnp.int32)
    padded = (counts + MOE_SUPER - 1) // MOE_SUPER * MOE_SUPER
    pad_end = jnp.cumsum(padded)
    pad_start = pad_end - padded
    dest = pad_start[idx] + pos
    n_live = pad_end[-1] // MOE_SUPER
    block_idx = jnp.minimum(jnp.arange(n_super, dtype=jnp.int32), n_live - 1)
    blk_start = block_idx * MOE_SUPER
    block_e = jnp.sum((pad_end[None, :] <= blk_start[:, None]).astype(jnp.int32), axis=1)
    block_e = jnp.minimum(block_e, N_EXPERTS - 1)
    live_rows = jnp.clip(counts[block_e] - (blk_start - pad_start[block_e]), 0, MOE_SUPER)
    nsub = (live_rows + MOE_SUB - 1) // MOE_SUB
    nsub = jnp.where(jnp.arange(n_super) < n_live, nsub, 0).astype(jnp.int32)
    fill0 = (pad_start + counts).astype(jnp.int32)
    nfill = ((counts + MOE_SUB - 1) // MOE_SUB * MOE_SUB - counts).astype(jnp.int32)
    return dest, block_e, nsub, block_idx, fill0, nfill


def kernel(x, c, ctx, c_ctx, w_mod, b_mod, norm_mix_g, norm_ffn_g, w_in, ssd_conv_w, ssd_conv_b, ssd_a_log_f, ssd_dt_bias_f, ssd_a_log_b, ssd_dt_bias_b, ssd_d_skip, ssd_norm_g, conf_dw_w, conf_dw_b, conf_ln_g, conf_ln_b, w_out, w_router, b_router, w_gate_up, b_gate_up, w_down, b_down, final_norm_g):
    bsz, n_lat, _ = x.shape
    n_ctx = ctx.shape[1]
    n_tok = bsz * n_lat
    l = 0

    c8 = jnp.zeros((8, D_MODEL), F32).at[:bsz].set(c).at[bsz].set(c_ctx)
    mod = _adaln_mod(c8, w_mod[l], b_mod[l][None, :])
    sh_m, sc_m, g_m, sh_f, sc_f, g_f = [mod[:bsz, k * D_MODEL:(k + 1) * D_MODEL].reshape(bsz, 1, D_MODEL)
                                        for k in range(N_MOD)]
    shc = mod[bsz, 0:D_MODEL].reshape(1, 1, D_MODEL)
    scc = mod[bsz, D_MODEL:2 * D_MODEL].reshape(1, 1, D_MODEL)

    wi = w_in[l]
    w_a = wi.astype(BF16)
    w_b = w_a[:, OFF_CONF:]
    wf = wi[:, OFF_DT:OFF_DT + SSD_HEADS].reshape(D_MODEL, SSD_GROUPS, HEADS_PER_GROUP)
    wb = wi[:, OFF_DT + SSD_HEADS:OFF_CONF].reshape(D_MODEL, SSD_GROUPS, HEADS_PER_GROUP)
    wz = jnp.zeros((D_MODEL, SSD_GROUPS, LANES - 2 * HEADS_PER_GROUP), F32)
    w_dt = jnp.concatenate([wf, wb, wz], axis=2).reshape(D_MODEL, SSD_GROUPS * LANES).astype(BF16)

    gain_mix = norm_mix_g[l][None, :]
    proj, dt_raw = _in_proj(x.reshape(n_tok, D_MODEL), gain_mix, sh_m, sc_m, w_a, w_b, w_dt,
                            0, PK_N // PROJ_TN, 1024, n_lat)
    n_cb = (D_SSD + 2 * D_BC) // PROJ_TN
    projc, dtc_raw = _in_proj(ctx.reshape(bsz * n_ctx, D_MODEL), gain_mix, shc, scc, w_a, w_b, w_dt,
                              PK_X // PROJ_TN, n_cb, bsz * n_ctx, bsz * n_ctx)
    proj3 = proj.reshape(bsz, n_lat, PK_N)

    ssd_out = _ssd(proj3, dt_raw.reshape(bsz, n_lat, -1), projc.reshape(bsz, n_ctx, -1),
                   dtc_raw.reshape(bsz, n_ctx, -1), ssd_conv_w[l], ssd_conv_b[l][None, :],
                   _pack_heads(ssd_dt_bias_f[l], ssd_dt_bias_b[l]), _pack_heads(ssd_a_log_f[l], ssd_a_log_b[l]),
                   jnp.repeat(ssd_d_skip[l], SSD_HEAD_DIM)[None, :], ssd_norm_g[l][None, :])

    conf_w = _conf_conv(proj3, conf_dw_w[l], conf_dw_b[l][None, :], True)
    conf_r = _conf_conv(proj3, conf_dw_w[l], conf_dw_b[l][None, :], False)

    wr = jnp.zeros((D_MODEL, LANES), F32).at[:, :N_EXPERTS].set(w_router[l])
    wr_hi = wr.astype(BF16)
    wr_lo = (wr - wr_hi.astype(F32)).astype(BF16)
    br = jnp.full((1, LANES), NEG_BIG, F32).at[0, :N_EXPERTS].set(b_router[l])
    half = D_CONF // 2
    x1, h2p, route, cnt = _out_proj(
        ssd_out.reshape(n_tok, D_SSD), conf_w.reshape(n_tok, half), conf_r.reshape(n_tok, half),
        x.reshape(n_tok, D_MODEL), w_out[l].astype(BF16), conf_ln_g[l][None, :], conf_ln_b[l][None, :], g_m,
        norm_ffn_g[l][None, :], sh_f, sc_f, wr_hi, wr_lo, br, n_lat)

    n_super = -(-n_tok * TOP_K // MOE_SUPER) + N_EXPERTS
    dest, block_e, nsub, block_idx, fill0, nfill = _routing_tables(route, cnt, n_super)
    dest3 = dest.reshape(n_tok // COMB_TM, 1, COMB_TM * TOP_K)
    xs = _dispatch(h2p, dest3, fill0, nfill, n_super * MOE_SUPER)
    y = _moe_mlp(xs, block_e, nsub, block_idx, w_gate_up[l], b_gate_up[l][:, None, :], w_down[l], b_down[l][:, None, :], n_super)
    out = _combine(dest3, y, x1, route, g_f, final_norm_g[None, :], n_lat)
    return out.reshape(bsz, n_lat, D_MODEL)
```
